```python
import jax, jax.numpy as jnp
from jax import lax
import numpy as np

D_MODEL = 1024
BATCH = 1
SEQ = 16384
DEPTH = 4

GRID_W = 64
CTX_LEN = 256
D_MIX = D_MODEL
HEAD_DIM = 64
POOL_GROUPS = 4
POOL_WINDOWS = (2, 4, 8, 16)
POOL_DIM = D_MIX // 4
POOL_GROUP_DIM = POOL_DIM // POOL_GROUPS
NA_HEADS = (D_MIX - POOL_DIM) // 2 // HEAD_DIM
NA_DIM = NA_HEADS * HEAD_DIM
NA_KH = 8
NA_KW = 16
RET_HEADS = (D_MIX - POOL_DIM - NA_DIM) // HEAD_DIM
RET_DIM = RET_HEADS * HEAD_DIM
RET_CHUNK = 128
D_FF = 2816
CONV_W = 3
ROPE_BASE = 10000.0
ROT_QUARTER = HEAD_DIM // 4
EPS = 1e-6
NEG_INF = -1e30
IN_SIZES = (POOL_DIM, NA_DIM, NA_DIM, NA_DIM, RET_DIM, RET_DIM, RET_DIM, RET_DIM)
IN_DIM = sum(IN_SIZES)
SPLIT_POINTS = tuple(int(s) for s in np.cumsum(IN_SIZES)[:-1])

kernel_name = "hybrid_pool_na_retention_dit"


def rmsnorm(t, g):
    tf = t.astype(jnp.float32)
    y = tf * lax.rsqrt(jnp.mean(tf * tf, axis=-1, keepdims=True) + EPS) * g.astype(jnp.float32)
    return y.astype(t.dtype)


def modulate(t, shift, scale):
    return t * (1 + scale) + shift


def heads(t, n):
    return t.reshape(t.shape[0], t.shape[1], n, HEAD_DIM)


def pool_mixer(p, w_grp, scale):
    B, L, _ = p.shape
    pf = p.astype(jnp.float32).reshape(B, L, POOL_GROUPS, POOL_GROUP_DIM)
    csum = jnp.concatenate([jnp.zeros_like(pf[:, :1]), jnp.cumsum(pf, axis=1)], axis=1)
    t = jnp.arange(L)[:, None]
    win = jnp.asarray(POOL_WINDOWS, dtype=jnp.int32)[None, :]
    lo = jnp.clip(t - win // 2, 0, L)
    hi = jnp.clip(t - win // 2 + win, 0, L)
    g = jnp.arange(POOL_GROUPS)[None, :]
    total = csum[:, hi, g] - csum[:, lo, g]
    mean = total / (hi - lo).astype(jnp.float32)[None, :, :, None]
    y = jnp.einsum('blgc,gcd->blgd', mean - pf, w_grp.astype(jnp.float32))
    return (y.reshape(B, L, POOL_DIM) * scale.astype(jnp.float32)).astype(p.dtype)


def neighborhood_attention(q, k, v, kc, vc, rpb, rows):
    B, L, H, dh = q.shape
    kh = min(NA_KH, rows)
    span = 2 * NA_KW
    n_cb = GRID_W // NA_KW
    qg = (q * dh ** -0.5).reshape(B, rows, GRID_W, H, dh)
    kg = k.reshape(B, rows, GRID_W, H, dh)
    vg = v.reshape(B, rows, GRID_W, H, dh)
    qcol = np.arange(GRID_W).reshape(n_cb, NA_KW)
    win_start = np.clip(qcol - NA_KW // 2, 0, GRID_W - NA_KW)
    col_start = np.clip(np.arange(n_cb) * NA_KW - NA_KW // 2, 0, GRID_W - span)
    kcol = col_start[:, None] + np.arange(span)
    col_valid = (kcol[:, None, :] >= win_start[:, :, None]) & (kcol[:, None, :] < win_start[:, :, None] + NA_KW)
    dc_idx = np.clip(kcol[:, None, :] - qcol[:, :, None] + NA_KW - 1, 0, 2 * NA_KW - 2)
    rpb_c = rpb.astype(jnp.float32)[:, :, dc_idx]
    valid = jnp.asarray(col_valid)[:, :, None, :]

    def one_row(r):
        rs = jnp.clip(r - kh // 2, 0, rows - kh)
        qr = lax.dynamic_index_in_dim(qg, r, axis=1, keepdims=False)
        kr = lax.dynamic_slice_in_dim(kg, rs, kh, axis=1)
        vr = lax.dynamic_slice_in_dim(vg, rs, kh, axis=1)
        kb = kr[:, :, kcol]
        vb = vr[:, :, kcol]
        qb = qr.reshape(B, n_cb, NA_KW, H, dh)
        dr_idx = rs + jnp.arange(kh) - r + NA_KH - 1
        bias = rpb_c[:, dr_idx].transpose(0, 2, 3, 1, 4)
        s_loc = jnp.einsum('bnqhd,binshd->bhnqis', qb, kb).astype(jnp.float32)
        s_loc = jnp.where(valid, s_loc + bias, NEG_INF)
        s_ctx = jnp.einsum('bnqhd,bkhd->bhnqk', qb, kc).astype(jnp.float32)
        s = jnp.concatenate([s_loc.reshape(B, H, n_cb, NA_KW, kh * span), s_ctx], axis=-1)
        p = jax.nn.softmax(s, axis=-1).astype(v.dtype)
        p_loc = p[..., :kh * span].reshape(B, H, n_cb, NA_KW, kh, span)
        p_ctx = p[..., kh * span:]
        o = jnp.einsum('bhnqis,binshd->bnqhd', p_loc, vb) + jnp.einsum('bhnqk,bkhd->bnqhd', p_ctx, vc)
        return o.reshape(B, GRID_W, H, dh)

    out = lax.map(one_row, jnp.arange(rows))
    return out.transpose(1, 0, 2, 3, 4).reshape(B, L, H, dh)


def context_attention(q, k, v):
    s = jnp.einsum('bqhd,bkhd->bhqk', q * HEAD_DIM ** -0.5, k).astype(jnp.float32)
    p = jax.nn.softmax(s, axis=-1).astype(v.dtype)
    return jnp.einsum('bhqk,bkhd->bqhd', p, v)


def axial_rotary(t, cos_r, sin_r, cos_c, sin_c):
    def rot(u, cos, sin):
        u1, u2 = jnp.split(u, 2, axis=-1)
        cos = cos[None, :, None, :]
        sin = sin[None, :, None, :]
        return jnp.concatenate([u1 * cos - u2 * sin, u1 * sin + u2 * cos], axis=-1)
    tr, tc = jnp.split(t, 2, axis=-1)
    return jnp.concatenate([rot(tr, cos_r, sin_r), rot(tc, cos_c, sin_c)], axis=-1)


def retention_chunks(q, k, v, log_gamma, state, with_outputs):
    B, L, H, _ = q.shape
    n = L // RET_CHUNK
    lg = log_gamma.astype(jnp.float32)
    idx = jnp.arange(RET_CHUNK, dtype=jnp.float32)
    diff = idx[:, None] - idx[None, :]
    intra = jnp.exp(jnp.where(diff[None] >= 0, diff[None] * lg[:, None, None], -jnp.inf))
    q_dec = jnp.exp((idx + 1)[None, :] * lg[:, None])
    k_dec = jnp.exp((RET_CHUNK - 1 - idx)[None, :] * lg[:, None])
    c_dec = jnp.exp(RET_CHUNK * lg)

    def to_chunks(t):
        return t.reshape(B, n, RET_CHUNK, H, t.shape[-1]).transpose(1, 0, 3, 2, 4)

    def step(s, blk):
        qb, kb, vb = blk
        s_new = s * c_dec[None, :, None, None] + jnp.einsum('bhcd,bhce->bhde', kb * k_dec[None, :, :, None], vb)
        if not with_outputs:
            return s_new, None
        a = jnp.einsum('bhid,bhjd->bhij', qb, kb) * intra[None]
        y = jnp.einsum('bhij,bhje->bhie', a, vb) + jnp.einsum('bhid,bhde->bhie', qb, s) * q_dec[None, :, :, None]
        return s_new, y

    s_fin, ys = lax.scan(step, state, (to_chunks(q), to_chunks(k), to_chunks(v)))
    if not with_outputs:
        return None, s_fin
    return ys.transpose(1, 0, 3, 2, 4).reshape(B, L, H, v.shape[-1]), s_fin


def bidirectional_retention(q, k, v, qc, kc, vc, dec_f, dec_b, need_ctx):
    B = q.shape[0]
    s0 = jnp.zeros((B, RET_HEADS, HEAD_DIM, HEAD_DIM), jnp.float32)
    rev = lambda t: jnp.flip(t, axis=1)
    yc_f, s_f = retention_chunks(qc, kc, vc, dec_f, s0, need_ctx)
    yx_f, _ = retention_chunks(q, k, v, dec_f, s_f, True)
    yc_b, s_b = retention_chunks(rev(qc), rev(kc), rev(vc), dec_b, s0, need_ctx)
    yx_b, _ = retention_chunks(rev(q), rev(k), rev(v), dec_b, s_b, True)
    yc = yc_f + rev(yc_b) if need_ctx else None
    return yx_f + rev(yx_b), yc


def retention_output(y, gate, gn_g):
    B, L = y.shape[:2]
    mu = jnp.mean(y, axis=-1, keepdims=True)
    var = jnp.mean((y - mu) ** 2, axis=-1, keepdims=True)
    yn = ((y - mu) * lax.rsqrt(var + EPS)).reshape(B, L, RET_DIM) * gn_g.astype(jnp.float32)
    return (jax.nn.silu(gate.astype(jnp.float32)) * yn).astype(gate.dtype)


def mixing_layer(px, pc, pool_w, pool_scale, na_rpb, dec_f, dec_b, gn_g, rope, rows, need_ctx):
    B, L, _ = px.shape
    xs = jnp.split(px, SPLIT_POINTS, axis=-1)
    cs = jnp.split(pc, SPLIT_POINTS, axis=-1)
    f32 = jnp.float32
    pool_x = pool_mixer(xs[0], pool_w, pool_scale)
    na_q, na_k, na_v = [heads(t, NA_HEADS) for t in xs[1:4]]
    nc_q, nc_k, nc_v = [heads(t, NA_HEADS) for t in cs[1:4]]
    na_x = neighborhood_attention(na_q, na_k, na_v, nc_k, nc_v, na_rpb, rows).reshape(B, L, NA_DIM)
    rq = axial_rotary(heads(xs[4], RET_HEADS).astype(f32), *rope)
    rk = axial_rotary(heads(xs[5], RET_HEADS).astype(f32), *rope) * HEAD_DIM ** -0.5
    rv = heads(xs[6], RET_HEADS).astype(f32)
    cq = heads(cs[4], RET_HEADS).astype(f32)
    ck = heads(cs[5], RET_HEADS).astype(f32) * HEAD_DIM ** -0.5
    cv = heads(cs[6], RET_HEADS).astype(f32)
    ret_x, ret_c = bidirectional_retention(rq, rk, rv, cq, ck, cv, dec_f, dec_b, need_ctx)
    ret_x = retention_output(ret_x, xs[7], gn_g)
    yx = jnp.concatenate([pool_x, na_x.astype(px.dtype), ret_x.astype(px.dtype)], axis=-1)
    if not need_ctx:
        return yx, None
    pool_c = pool_mixer(cs[0], pool_w, pool_scale)
    na_c = context_attention(nc_q, nc_k, nc_v).reshape(B, pc.shape[1], NA_DIM)
    ret_c = retention_output(ret_c, cs[7], gn_g)
    yc = jnp.concatenate([pool_c, na_c.astype(pc.dtype), ret_c.astype(pc.dtype)], axis=-1)
    return yx, yc


def conv_ffn(h, w_up, conv_w, conv_b, w_down):
    u = h @ w_up
    up = jnp.pad(u, ((0, 0), (1, 1), (0, 0)))
    u = up[:, :-2] * conv_w[0] + up[:, 1:-1] * conv_w[1] + up[:, 2:] * conv_w[2] + conv_b
    a, b = jnp.split(u, 2, axis=-1)
    return (jax.nn.silu(a) * b) @ w_down


def setup_inputs(seed: int = 0) -> dict:
    key = jax.random.key(seed)
    ks = jax.random.split(key, 24)
    f32 = jnp.float32
    nrm = lambda k, shape, s: jax.random.normal(k, shape, f32) * s
    exps_f = 5.0 + jnp.arange(RET_HEADS, dtype=f32)[None, :] + 0.25 * jax.random.uniform(ks[11], (DEPTH, RET_HEADS), f32)
    exps_b = 5.0 + jnp.arange(RET_HEADS, dtype=f32)[None, :] + 0.25 * jax.random.uniform(ks[12], (DEPTH, RET_HEADS), f32)
    return {
        "x": nrm(ks[0], (BATCH, SEQ, D_MODEL), 1.0),
        "c": nrm(ks[1], (BATCH, D_MODEL), 1.0),
        "ctx": nrm(ks[2], (BATCH, CTX_LEN, D_MODEL), 1.0),
        "c_ctx": nrm(ks[3], (D_MODEL,), 1.0),
        "w_mod": nrm(ks[4], (DEPTH, D_MODEL, 6 * D_MODEL), 0.5 * D_MODEL ** -0.5),
        "b_mod": nrm(ks[5], (DEPTH, 6 * D_MODEL), 0.02),
        "norm1_g": 1.0 + nrm(ks[6], (DEPTH, D_MODEL), 0.02),
        "w_in": nrm(ks[7], (DEPTH, D_MODEL, IN_DIM), D_MODEL ** -0.5),
        "pool_w": nrm(ks[8], (DEPTH, POOL_GROUPS, POOL_GROUP_DIM, POOL_GROUP_DIM), POOL_GROUP_DIM ** -0.5),
        "pool_scale": 1.0 + nrm(ks[9], (DEPTH, POOL_DIM), 0.1),
        "na_rpb": nrm(ks[10], (DEPTH, NA_HEADS, 2 * NA_KH - 1, 2 * NA_KW - 1), 0.1),
        "ret_decay_fwd": jnp.log1p(-jnp.exp2(-exps_f)),
        "ret_decay_bwd": jnp.log1p(-jnp.exp2(-exps_b)),
        "ret_gn_g": 1.0 + nrm(ks[13], (DEPTH, RET_DIM), 0.02),
        "w_out": nrm(ks[14], (DEPTH, D_MIX, D_MODEL), D_MIX ** -0.5),
        "norm2_g": 1.0 + nrm(ks[15], (DEPTH, D_MODEL), 0.02),
        "w_up": nrm(ks[16], (DEPTH, D_MODEL, 2 * D_FF), D_MODEL ** -0.5),
        "conv_w": nrm(ks[17], (DEPTH, CONV_W, 2 * D_FF), CONV_W ** -0.5),
        "conv_b": nrm(ks[18], (DEPTH, 2 * D_FF), 0.02),
        "w_down": nrm(ks[19], (DEPTH, D_FF, D_MODEL), D_FF ** -0.5),
        "final_g": 1.0 + nrm(ks[20], (D_MODEL,), 0.02),
    }


def reference(x, c, ctx, c_ctx, w_mod, b_mod, norm1_g, w_in, pool_w, pool_scale, na_rpb, ret_decay_fwd, ret_decay_bwd, ret_gn_g, w_out, norm2_g, w_up, conv_w, conv_b, w_down, final_g):
    B, L, _ = x.shape
    rows = L // GRID_W
    pos = jnp.arange(L)
    inv = ROPE_BASE ** (-jnp.arange(ROT_QUARTER, dtype=jnp.float32) / ROT_QUARTER)
    ang_r = (pos // GRID_W).astype(jnp.float32)[:, None] * inv[None, :]
    ang_c = (pos % GRID_W).astype(jnp.float32)[:, None] * inv[None, :]
    rope = (jnp.cos(ang_r), jnp.sin(ang_r), jnp.cos(ang_c), jnp.sin(ang_c))
    silu_c = jax.nn.silu(c)[:, None, :]
    silu_cc = jax.nn.silu(c_ctx)
    h = x
    hc = ctx
    for l in range(DEPTH):
        need_ctx = l < DEPTH - 1
        mx = jnp.split(silu_c @ w_mod[l] + b_mod[l], 6, axis=-1)
        mc = jnp.split(silu_cc @ w_mod[l] + b_mod[l], 6, axis=-1)
        ax = modulate(rmsnorm(h, norm1_g[l]), mx[0], mx[1])
        ac = modulate(rmsnorm(hc, norm1_g[l]), mc[0], mc[1])
        yx, yc = mixing_layer(ax @ w_in[l], ac @ w_in[l], pool_w[l], pool_scale[l], na_rpb[l],
                              ret_decay_fwd[l], ret_decay_bwd[l], ret_gn_g[l], rope, rows, need_ctx)
        h = h + mx[2] * (yx @ w_out[l])
        h = h + mx[5] * conv_ffn(modulate(rmsnorm(h, norm2_g[l]), mx[3], mx[4]), w_up[l], conv_w[l], conv_b[l], w_down[l])
        if need_ctx:
            hc = hc + mc[2] * (yc @ w_out[l])
            hc = hc + mc[5] * conv_ffn(modulate(rmsnorm(hc, norm2_g[l]), mc[3], mc[4]), w_up[l], conv_w[l], conv_b[l], w_down[l])
    return rmsnorm(h, final_g)
```

```python
import functools

import numpy as np
import jax
import jax.numpy as jnp
from jax import lax
from jax.experimental import pallas as pl
from jax.experimental.pallas import tpu as pltpu

f32 = jnp.float32
bf16 = jnp.bfloat16

D_MODEL = 1024
DEPTH = 4
GRID_W = 64
HEAD_DIM = 64
POOL_GROUPS = 4
POOL_WINDOWS = (2, 4, 8, 16)
POOL_DIM = 256
NA_HEADS = 6
NA_DIM = 384
NA_KH = 8
NA_KW = 16
RET_HEADS = 6
RET_DIM = 384
RET_CHUNK = 128
D_FF = 2816
ROPE_BASE = 10000.0
ROT_QUARTER = HEAD_DIM // 4
EPS = 1e-6
NEG_INF = -1e30
IN_DIM = POOL_DIM + 3 * NA_DIM + 4 * RET_DIM
OFF_NA = POOL_DIM
OFF_RET = OFF_NA + 3 * NA_DIM
OFF_GATE = OFF_RET + 3 * RET_DIM

LANES = 128
N_PAIRS = NA_HEADS // 2
VMEM_LIMIT = 56 * 1024 * 1024

NT_DIMS = (((1,), (1,)), ((), ()))
TN_DIMS = (((0,), (0,)), ((), ()))


def _params(*sem):
    return pltpu.CompilerParams(dimension_semantics=sem, vmem_limit_bytes=VMEM_LIMIT)


def _row_tile(L):
    return min(L, 512)


def _silu(x):
    return x * jax.nn.sigmoid(x)


def _lo_half():
    return lax.broadcasted_iota(jnp.int32, (1, LANES), 1) < HEAD_DIM


def _mod_kernel(s_ref, w_ref, b_ref, o_ref):
    a = _silu(s_ref[...]).astype(bf16)
    o_ref[...] = jnp.dot(a, w_ref[...].astype(bf16), preferred_element_type=f32) + b_ref[...]


def mod_vectors(c, c_ctx, w_mod, b_mod):
    s = jnp.zeros((8, D_MODEL), f32).at[0].set(c[0]).at[1].set(c_ctx)
    return pl.pallas_call(
        _mod_kernel,
        grid=(DEPTH, 6),
        in_specs=[
            pl.BlockSpec((8, D_MODEL), lambda l, j: (0, 0)),
            pl.BlockSpec((None, D_MODEL, D_MODEL), lambda l, j: (l, 0, j)),
            pl.BlockSpec((None, 1, D_MODEL), lambda l, j: (l, 0, j)),
        ],
        out_specs=pl.BlockSpec((None, 8, D_MODEL), lambda l, j: (l, 0, j)),
        out_shape=jax.ShapeDtypeStruct((DEPTH, 8, 6 * D_MODEL), f32),
        compiler_params=_params("parallel", "parallel"),
        name="mod_vectors",
    )(s, w_mod, b_mod.reshape(DEPTH, 1, 6 * D_MODEL))


def _in_proj_kernel(h_ref, g_ref, mod_ref, w_ref, cos_ref, sin_ref, p_ref, na_ref, ret_ref, gate_ref):
    h = h_ref[...]
    y = h * lax.rsqrt(jnp.mean(h * h, axis=-1, keepdims=True) + EPS) * g_ref[...]
    a = (y * (1.0 + mod_ref[1:2, :]) + mod_ref[0:1, :]).astype(bf16)

    def proj(lo, n):
        return jnp.dot(a, w_ref[:, lo:lo + n], preferred_element_type=f32)

    p_ref[...] = proj(0, POOL_DIM)
    for s in range(3):
        seg = proj(OFF_NA + NA_DIM * s, NA_DIM)
        if s == 0:
            seg = seg * HEAD_DIM ** -0.5
        na_ref[:, NA_DIM * s:NA_DIM * (s + 1)] = seg.astype(bf16)
    first = lax.broadcasted_iota(jnp.int32, (1, LANES), 1) % (2 * ROT_QUARTER) < ROT_QUARTER
    cos = cos_ref[...]
    sin = sin_ref[...]
    for s in range(2):
        for slab in range(RET_DIM // LANES):
            lo = RET_DIM * s + LANES * slab
            x = proj(OFF_RET + lo, LANES)
            partner = jnp.where(first, pltpu.roll(x, LANES - ROT_QUARTER, 1), pltpu.roll(x, ROT_QUARTER, 1))
            x = x * cos + partner * sin
            if s == 1:
                x = x * HEAD_DIM ** -0.5
            ret_ref[:, lo:lo + LANES] = x.astype(bf16)
    ret_ref[:, 2 * RET_DIM:] = proj(OFF_RET + 2 * RET_DIM, RET_DIM).astype(bf16)
    gate_ref[...] = proj(OFF_GATE, RET_DIM)


def in_proj(h, g, mod, w_in, cos, sin):
    L = h.shape[0]
    tm = _row_tile(L)
    row = lambda n: pl.BlockSpec((tm, n), lambda i: (i, 0))
    full = lambda a: pl.BlockSpec(a.shape, lambda i: (0,) * a.ndim)
    return pl.pallas_call(
        _in_proj_kernel,
        grid=(L // tm,),
        in_specs=[row(D_MODEL), full(g), full(mod), full(w_in), row(LANES), row(LANES)],
        out_specs=[row(POOL_DIM), row(3 * NA_DIM), row(3 * RET_DIM), row(RET_DIM)],
        out_shape=[
            jax.ShapeDtypeStruct((L, POOL_DIM), f32),
            jax.ShapeDtypeStruct((L, 3 * NA_DIM), bf16),
            jax.ShapeDtypeStruct((L, 3 * RET_DIM), bf16),
            jax.ShapeDtypeStruct((L, RET_DIM), f32),
        ],
        compiler_params=_params("parallel"),
        name="in_proj",
    )(h, g, mod, w_in, cos, sin)


def rope_tables(L):
    pos = jnp.arange(L)
    inv = ROPE_BASE ** (-jnp.arange(ROT_QUARTER, dtype=f32) / ROT_QUARTER)
    ang_r = (pos // GRID_W).astype(f32)[:, None] * inv[None, :]
    ang_c = (pos % GRID_W).astype(f32)[:, None] * inv[None, :]
    cos = jnp.concatenate([jnp.cos(ang_r)] * 2 + [jnp.cos(ang_c)] * 2, axis=-1)
    sin = jnp.concatenate([-jnp.sin(ang_r), jnp.sin(ang_r), -jnp.sin(ang_c), jnp.sin(ang_c)], axis=-1)
    return jnp.tile(cos, (1, 2)), jnp.tile(sin, (1, 2))


POOL_HALO = 8


def _pool_kernel(prev_ref, x_ref, next_ref, w_ref, scale_ref, o_ref, ext_ref, *, tm, L):
    i = pl.program_id(0)
    n = pl.num_programs(0)
    ext_ref[0:POOL_HALO, :] = jnp.where(i > 0, prev_ref[...], 0.0)
    ext_ref[POOL_HALO:POOL_HALO + tm, :] = x_ref[...]
    ext_ref[POOL_HALO + tm:, :] = jnp.where(i < n - 1, next_ref[...], 0.0)

    def sh(d):
        return ext_ref[POOL_HALO + d:POOL_HALO + d + tm, :]

    x = x_ref[...]
    sums = []
    acc = None
    prev_w = 0
    for w in POOL_WINDOWS:
        for d in list(range(-(w // 2), -(prev_w // 2))) + list(range(prev_w // 2, w // 2)):
            term = x if d == 0 else sh(d)
            acc = term if acc is None else acc + term
        sums.append(acc)
        prev_w = w
    lane = lax.broadcasted_iota(jnp.int32, (1, POOL_DIM), 1)
    t = i * tm + lax.broadcasted_iota(jnp.int32, (tm, 1), 0)
    gdim = POOL_DIM // POOL_GROUPS
    total = sums[-1]
    count = None
    for gi in range(POOL_GROUPS - 1, -1, -1):
        w = POOL_WINDOWS[gi]
        c = (jnp.minimum(t - w // 2 + w, L) - jnp.maximum(t - w // 2, 0)).astype(f32)
        if count is None:
            count = jnp.broadcast_to(c, (tm, POOL_DIM))
        else:
            sel = lane < gdim * (gi + 1)
            total = jnp.where(sel, sums[gi], total)
            count = jnp.where(sel, c, count)
    m = (total / count - x).astype(bf16)
    y = jnp.dot(m, w_ref[...], preferred_element_type=f32) * scale_ref[...]
    o_ref[...] = y.astype(bf16)


def pool_mixer(p, w_blockdiag, scale):
    L = p.shape[0]
    tm = _row_tile(L)
    hb = tm // POOL_HALO
    return pl.pallas_call(
        functools.partial(_pool_kernel, tm=tm, L=L),
        grid=(L // tm,),
        in_specs=[
            pl.BlockSpec((POOL_HALO, POOL_DIM), lambda i: (jnp.maximum(i * hb - 1, 0), 0)),
            pl.BlockSpec((tm, POOL_DIM), lambda i: (i, 0)),
            pl.BlockSpec((POOL_HALO, POOL_DIM), lambda i: (jnp.minimum((i + 1) * hb, L // POOL_HALO - 1), 0)),
            pl.BlockSpec((POOL_DIM, POOL_DIM), lambda i: (0, 0)),
            pl.BlockSpec((1, POOL_DIM), lambda i: (0, 0)),
        ],
        out_specs=pl.BlockSpec((tm, POOL_DIM), lambda i: (i, 0)),
        out_shape=jax.ShapeDtypeStruct((L, POOL_DIM), bf16),
        scratch_shapes=[pltpu.VMEM((tm + 2 * POOL_HALO, POOL_DIM), f32)],
        compiler_params=_params("parallel"),
        name="pool_mixer",
    )(p, p, p, w_blockdiag, scale)


def pool_blockdiag(pool_w):
    gdim = POOL_DIM // POOL_GROUPS
    w = jnp.zeros((POOL_DIM, POOL_DIM), f32)
    for g in range(POOL_GROUPS):
        w = w.at[g * gdim:(g + 1) * gdim, g * gdim:(g + 1) * gdim].set(pool_w[g])
    return w.astype(bf16)


NA_RB = 8
NA_QT = NA_RB * GRID_W
NA_KT = NA_KH * GRID_W


def na_bias_table(rpb):
    qcol = np.arange(GRID_W)[:, None]
    kcol = np.arange(GRID_W)[None, :]
    ws = np.clip(qcol - NA_KW // 2, 0, GRID_W - NA_KW)
    valid = (kcol >= ws) & (kcol < ws + NA_KW)
    dc = np.clip(kcol - qcol + NA_KW - 1, 0, 2 * NA_KW - 2)
    o = np.arange(NA_KH)[:, None]
    i = np.arange(NA_KH)[None, :]
    dr = i - o + NA_KH - 1
    t = rpb.astype(f32)[:, dr][:, :, :, dc]
    t = jnp.where(jnp.asarray(valid)[None, None, None], t, NEG_INF)
    return t.transpose(1, 0, 3, 2, 4).reshape(NA_KH, NA_HEADS, GRID_W, NA_KT)


def _softmax_pv(s_parts, v_parts):
    m = functools.reduce(jnp.maximum, [jnp.max(s, axis=-1, keepdims=True) for s in s_parts])
    den = None
    out = None
    for s, v in zip(s_parts, v_parts):
        p = jnp.exp(s - m)
        ps = jnp.sum(p, axis=-1, keepdims=True)
        den = ps if den is None else den + ps
        pv = jnp.dot(p.astype(bf16), v, preferred_element_type=f32)
        out = pv if out is None else out + pv
    return out / den


def _na_kernel(q_ref, k0_ref, k1_ref, k2_ref, v0_ref, v1_ref, v2_ref, kc_ref, vc_ref, bias_ref, o_ref,
               kbuf, vbuf, *, rows):
    b = pl.program_id(0)
    for t, (kr, vr) in enumerate(((k0_ref, v0_ref), (k1_ref, v1_ref), (k2_ref, v2_ref))):
        kbuf[t * NA_QT:(t + 1) * NA_QT, :] = kr[...]
        vbuf[t * NA_QT:(t + 1) * NA_QT, :] = vr[...]
    lo_half = _lo_half()

    def row_body(j, carry):
        r = b * NA_RB + j
        rs = jnp.clip(r - NA_KH // 2, 0, rows - NA_KH)
        off = pl.multiple_of((rs - b * NA_RB + NA_RB) * GRID_W, GRID_W)
        o = r - rs
        q0 = pl.multiple_of(j * GRID_W, GRID_W)
        for hp in range(N_PAIRS):
            sl = slice(LANES * hp, LANES * (hp + 1))
            q2 = q_ref[pl.ds(q0, GRID_W), sl]
            k2 = kbuf[pl.ds(off, NA_KT), sl]
            v2 = vbuf[pl.ds(off, NA_KT), sl]
            kc2 = kc_ref[:, sl]
            vc2 = vc_ref[:, sl]
            outs = []
            for sub in range(2):
                qm = jnp.where(lo_half if sub == 0 else jnp.logical_not(lo_half), q2, jnp.zeros_like(q2))
                s = lax.dot_general(qm, k2, NT_DIMS, preferred_element_type=f32)
                tb = bias_ref[o, 2 * hp + sub]
                s = jnp.where(tb < 0.5 * NEG_INF, NEG_INF, s + tb)
                sc = lax.dot_general(qm, kc2, NT_DIMS, preferred_element_type=f32)
                outs.append(_softmax_pv([s, sc], [v2, vc2]))
            o_ref[pl.ds(q0, GRID_W), sl] = jnp.where(lo_half, outs[0], outs[1]).astype(bf16)
        return carry

    lax.fori_loop(0, NA_RB, row_body, 0)


def neighborhood_attention(na, na_ctx, bias):
    L = na.shape[0]
    rows = L // GRID_W
    nb = rows // NA_RB
    assert rows % NA_RB == 0 and nb >= 2
    ctx = na_ctx.shape[0]
    blk = lambda f, col: pl.BlockSpec((NA_QT, NA_DIM), lambda b: (f(b), col))
    prev = lambda b: jnp.maximum(b - 1, 0)
    cur = lambda b: b
    nxt = lambda b: jnp.minimum(b + 1, nb - 1)
    return pl.pallas_call(
        functools.partial(_na_kernel, rows=rows),
        grid=(nb,),
        in_specs=[
            blk(cur, 0),
            blk(prev, 1), blk(cur, 1), blk(nxt, 1),
            blk(prev, 2), blk(cur, 2), blk(nxt, 2),
            pl.BlockSpec((ctx, NA_DIM), lambda b: (0, 1)),
            pl.BlockSpec((ctx, NA_DIM), lambda b: (0, 2)),
            pl.BlockSpec(bias.shape, lambda b: (0, 0, 0, 0)),
        ],
        out_specs=pl.BlockSpec((NA_QT, NA_DIM), lambda b: (b, 0)),
        out_shape=jax.ShapeDtypeStruct((L, NA_DIM), bf16),
        scratch_shapes=[pltpu.VMEM((3 * NA_QT, NA_DIM), bf16), pltpu.VMEM((3 * NA_QT, NA_DIM), bf16)],
        compiler_params=_params("parallel"),
        name="neighborhood_attention",
    )(na, na, na, na, na, na, na, na_ctx, na_ctx, bias)


def _ctx_attn_kernel(q_ref, k_ref, v_ref, o_ref):
    lo_half = _lo_half()
    for hp in range(N_PAIRS):
        sl = slice(LANES * hp, LANES * (hp + 1))
        q2 = q_ref[:, sl]
        k2 = k_ref[:, sl]
        v2 = v_ref[:, sl]
        outs = []
        for sub in range(2):
            qm = jnp.where(lo_half if sub == 0 else jnp.logical_not(lo_half), q2, jnp.zeros_like(q2))
            s = lax.dot_general(qm, k2, NT_DIMS, preferred_element_type=f32)
            outs.append(_softmax_pv([s], [v2]))
        o_ref[:, sl] = jnp.where(lo_half, outs[0], outs[1]).astype(bf16)


def context_attention(na_ctx):
    ctx = na_ctx.shape[0]
    col = lambda c: pl.BlockSpec((ctx, NA_DIM), lambda i: (0, c))
    return pl.pallas_call(
        _ctx_attn_kernel,
        grid=(1,),
        in_specs=[col(0), col(1), col(2)],
        out_specs=pl.BlockSpec((ctx, NA_DIM), lambda i: (0, 0)),
        out_shape=jax.ShapeDtypeStruct((ctx, NA_DIM), bf16),
        compiler_params=_params("arbitrary"),
        name="context_attention",
    )(na_ctx, na_ctx, na_ctx)


def _pair_blockmask():
    r = lax.broadcasted_iota(jnp.int32, (LANES, LANES), 0) < HEAD_DIM
    c = lax.broadcasted_iota(jnp.int32, (LANES, LANES), 1) < HEAD_DIM
    return r == c


def _ret_state_kernel(kf_ref, vf_ref, kb_ref, vb_ref, lgf_ref, lgb_ref, s0f_ref, s0b_ref,
                      sf_out, sb_out, sf_fin, sb_fin, sf_scr, sb_scr):
    n = pl.program_id(0)

    @pl.when(n == 0)
    def _():
        sf_scr[...] = s0f_ref[...]
        sb_scr[...] = s0b_ref[...]

    c = lax.broadcasted_iota(jnp.int32, (RET_CHUNK, 1), 0).astype(f32)
    blockmask = _pair_blockmask()
    dirs = (
        (kf_ref, vf_ref, lgf_ref, sf_scr, sf_out, sf_fin, RET_CHUNK - 1.0 - c),
        (kb_ref, vb_ref, lgb_ref, sb_scr, sb_out, sb_fin, c),
    )
    for k_ref, v_ref, lg_ref, scr, out, fin, steps in dirs:
        lg = lg_ref[...]
        kd = (k_ref[...].astype(f32) * jnp.exp(steps * lg)).astype(bf16)
        v = v_ref[...]
        cdec = jnp.exp(float(RET_CHUNK) * lg)
        for hp in range(N_PAIRS):
            sl = slice(LANES * hp, LANES * (hp + 1))
            s_prev = scr[hp]
            out[hp] = s_prev
            kv = lax.dot_general(kd[:, sl], v[:, sl], TN_DIMS, preferred_element_type=f32)
            s_new = s_prev * cdec[:, sl] + jnp.where(blockmask, kv, 0.0)
            scr[hp] = s_new
            fin[hp] = s_new


def retention_states(ret, lgf_lane, lgb_lane, s0f, s0b):
    L = ret.shape[0]
    N = L // RET_CHUNK
    blk = lambda f, col: pl.BlockSpec((RET_CHUNK, RET_DIM), lambda n: (f(n), col))
    fwd = lambda n: n
    bwd = lambda n: N - 1 - n
    vec = pl.BlockSpec((1, RET_DIM), lambda n: (0, 0))
    st = pl.BlockSpec((N_PAIRS, LANES, LANES), lambda n: (0, 0, 0))
    st_shape = jax.ShapeDtypeStruct((N_PAIRS, LANES, LANES), f32)
    all_shape = jax.ShapeDtypeStruct((N, N_PAIRS, LANES, LANES), f32)
    return pl.pallas_call(
        _ret_state_kernel,
        grid=(N,),
        in_specs=[blk(fwd, 1), blk(fwd, 2), blk(bwd, 1), blk(bwd, 2), vec, vec, st, st],
        out_specs=[
            pl.BlockSpec((None, N_PAIRS, LANES, LANES), lambda n: (n, 0, 0, 0)),
            pl.BlockSpec((None, N_PAIRS, LANES, LANES), lambda n: (N - 1 - n, 0, 0, 0)),
            st, st,
        ],
        out_shape=[all_shape, all_shape, st_shape, st_shape],
        scratch_shapes=[pltpu.VMEM((N_PAIRS, LANES, LANES), f32), pltpu.VMEM((N_PAIRS, LANES, LANES), f32)],
        compiler_params=_params("arbitrary"),
        name="retention_states",
    )(ret, ret, ret, ret, lgf_lane, lgb_lane, s0f, s0b)


def _ret_out_kernel(lgf_s, lgb_s, q_ref, k_ref, v_ref, gate_ref, sf_ref, sb_ref, lgf_ref, lgb_ref, gn_ref, o_ref):
    lo_half = _lo_half()
    diff = (lax.broadcasted_iota(jnp.int32, (RET_CHUNK, RET_CHUNK), 0)
            - lax.broadcasted_iota(jnp.int32, (RET_CHUNK, RET_CHUNK), 1)).astype(f32)
    c = lax.broadcasted_iota(jnp.int32, (RET_CHUNK, 1), 0).astype(f32)
    q = q_ref[...]
    qf32 = q.astype(f32)
    q_f = (qf32 * jnp.exp((c + 1.0) * lgf_ref[...])).astype(bf16)
    q_b = (qf32 * jnp.exp((RET_CHUNK - c) * lgb_ref[...])).astype(bf16)
    inv_hd = 1.0 / HEAD_DIM
    for hp in range(N_PAIRS):
        sl = slice(LANES * hp, LANES * (hp + 1))
        q2 = q[:, sl]
        k2 = k_ref[:, sl]
        v2 = v_ref[:, sl]
        lhs = []
        rhs = []
        for sub in range(2):
            h = 2 * hp + sub
            hmask = lo_half if sub == 0 else jnp.logical_not(lo_half)
            qm = jnp.where(hmask, q2, jnp.zeros_like(q2))
            a = lax.dot_general(qm, k2, NT_DIMS, preferred_element_type=f32)
            decay = (jnp.where(diff >= 0, jnp.exp(diff * lgf_s[h]), 0.0)
                     + jnp.where(diff <= 0, jnp.exp(-diff * lgb_s[h]), 0.0))
            lhs.append((a * decay).astype(bf16))
            rhs.append(jnp.where(hmask, v2, jnp.zeros_like(v2)))
        lhs += [q_f[:, sl], q_b[:, sl]]
        rhs += [sf_ref[hp].astype(bf16), sb_ref[hp].astype(bf16)]
        y = jnp.dot(jnp.concatenate(lhs, axis=1), jnp.concatenate(rhs, axis=0), preferred_element_type=f32)

        def head_mean(t):
            m0 = jnp.sum(jnp.where(lo_half, t, 0.0), axis=-1, keepdims=True) * inv_hd
            m1 = jnp.sum(jnp.where(lo_half, 0.0, t), axis=-1, keepdims=True) * inv_hd
            return jnp.where(lo_half, m0, m1)

        d = y - head_mean(y)
        yn = d * lax.rsqrt(head_mean(d * d) + EPS) * gn_ref[:, sl]
        o_ref[:, sl] = (_silu(gate_ref[:, sl]) * yn).astype(bf16)


def retention_outputs(ret, gate, sf, sb, lgf, lgb, lgf_lane, lgb_lane, gn):
    L = ret.shape[0]
    N = L // RET_CHUNK
    col = lambda c: pl.BlockSpec((RET_CHUNK, RET_DIM), lambda n, *_: (n, c))
    vec = pl.BlockSpec((1, RET_DIM), lambda n, *_: (0, 0))
    st = pl.BlockSpec((None, N_PAIRS, LANES, LANES), lambda n, *_: (n, 0, 0, 0))
    return pl.pallas_call(
        _ret_out_kernel,
        grid_spec=pltpu.PrefetchScalarGridSpec(
            num_scalar_prefetch=2,
            grid=(N,),
            in_specs=[col(0), col(1), col(2), pl.BlockSpec((RET_CHUNK, RET_DIM), lambda n, *_: (n, 0)),
                      st, st, vec, vec, vec],
            out_specs=pl.BlockSpec((RET_CHUNK, RET_DIM), lambda n, *_: (n, 0)),
        ),
        out_shape=jax.ShapeDtypeStruct((L, RET_DIM), bf16),
        compiler_params=_params("parallel"),
        name="retention_outputs",
    )(lgf, lgb, ret, ret, ret, gate, sf, sb, lgf_lane, lgb_lane, gn)


def _out_proj_kernel(h_ref, pool_ref, na_ref, ret_ref, w_ref, mod_ref, g_ref, h_out, a_out):
    y = jnp.dot(pool_ref[...], w_ref[0:POOL_DIM, :], preferred_element_type=f32)
    y = y + jnp.dot(na_ref[...], w_ref[POOL_DIM:POOL_DIM + NA_DIM, :], preferred_element_type=f32)
    y = y + jnp.dot(ret_ref[...], w_ref[POOL_DIM + NA_DIM:, :], preferred_element_type=f32)
    h = h_ref[...] + mod_ref[2:3, :] * y
    h_out[...] = h
    n = h * lax.rsqrt(jnp.mean(h * h, axis=-1, keepdims=True) + EPS) * g_ref[...]
    a_out[...] = (n * (1.0 + mod_ref[4:5, :]) + mod_ref[3:4, :]).astype(bf16)


def out_proj(h, pool_x, na_x, ret_x, w_out, mod, g):
    L = h.shape[0]
    tm = _row_tile(L)
    row = lambda n: pl.BlockSpec((tm, n), lambda i: (i, 0))
    full = lambda a: pl.BlockSpec(a.shape, lambda i: (0,) * a.ndim)
    return pl.pallas_call(
        _out_proj_kernel,
        grid=(L // tm,),
        in_specs=[row(D_MODEL), row(POOL_DIM), row(NA_DIM), row(RET_DIM), full(w_out), full(mod), full(g)],
        out_specs=[row(D_MODEL), row(D_MODEL)],
        out_shape=[jax.ShapeDtypeStruct((L, D_MODEL), f32), jax.ShapeDtypeStruct((L, D_MODEL), bf16)],
        compiler_params=_params("parallel"),
        name="out_proj",
    )(h, pool_x, na_x, ret_x, w_out, mod, g)


FFN_HALO = 16
FFN_TF = 256


def _ffn_kernel(aprev_ref, a_ref, anext_ref, h_ref, mod_ref, wa_ref, wb_ref, cwa_ref, cwb_ref, cba_ref, cbb_ref,
                wd_ref, o_ref, aext, acc, *, tm):
    i = pl.program_id(0)
    j = pl.program_id(1)
    ni = pl.num_programs(0)
    nj = pl.num_programs(1)

    @pl.when(j == 0)
    def _():
        aext[0:FFN_HALO, :] = jnp.where(i > 0, aprev_ref[...], jnp.zeros_like(aprev_ref))
        aext[FFN_HALO:FFN_HALO + tm, :] = a_ref[...]
        aext[FFN_HALO + tm:, :] = jnp.where(i < ni - 1, anext_ref[...], jnp.zeros_like(anext_ref))
        acc[...] = jnp.zeros_like(acc)

    ae = aext[...]
    ext = tm + 2 * FFN_HALO

    def branch(w_ref, cw_ref, cb_ref):
        u = jnp.dot(ae, w_ref[...], preferred_element_type=f32)
        up = pltpu.roll(u, 1, 0)[FFN_HALO:FFN_HALO + tm]
        un = pltpu.roll(u, ext - 1, 0)[FFN_HALO:FFN_HALO + tm]
        uc = u[FFN_HALO:FFN_HALO + tm]
        return up * cw_ref[0:1, :] + uc * cw_ref[1:2, :] + un * cw_ref[2:3, :] + cb_ref[...]

    ca = branch(wa_ref, cwa_ref, cba_ref)
    cb = branch(wb_ref, cwb_ref, cbb_ref)
    g = (_silu(ca) * cb).astype(bf16)
    acc[...] += jnp.dot(g, wd_ref[...], preferred_element_type=f32)

    @pl.when(j == nj - 1)
    def _():
        o_ref[...] = h_ref[...] + mod_ref[5:6, :] * acc[...]


def conv_ffn(a, h, mod, w_up, conv_w, conv_b, w_down):
    L = h.shape[0]
    tm = _row_tile(L)
    tf = FFN_TF
    nf = D_FF // tf
    hb = tm // FFN_HALO
    return pl.pallas_call(
        functools.partial(_ffn_kernel, tm=tm),
        grid=(L // tm, nf),
        in_specs=[
            pl.BlockSpec((FFN_HALO, D_MODEL), lambda i, j: (jnp.maximum(i * hb - 1, 0), 0)),
            pl.BlockSpec((tm, D_MODEL), lambda i, j: (i, 0)),
            pl.BlockSpec((FFN_HALO, D_MODEL), lambda i, j: (jnp.minimum((i + 1) * hb, L // FFN_HALO - 1), 0)),
            pl.BlockSpec((tm, D_MODEL), lambda i, j: (i, 0)),
            pl.BlockSpec(mod.shape, lambda i, j: (0, 0)),
            pl.BlockSpec((D_MODEL, tf), lambda i, j: (0, j)),
            pl.BlockSpec((D_MODEL, tf), lambda i, j: (0, nf + j)),
            pl.BlockSpec((3, tf), lambda i, j: (0, j)),
            pl.BlockSpec((3, tf), lambda i, j: (0, nf + j)),
            pl.BlockSpec((1, tf), lambda i, j: (0, j)),
            pl.BlockSpec((1, tf), lambda i, j: (0, nf + j)),
            pl.BlockSpec((tf, D_MODEL), lambda i, j: (j, 0)),
        ],
        out_specs=pl.BlockSpec((tm, D_MODEL), lambda i, j: (i, 0)),
        out_shape=jax.ShapeDtypeStruct((L, D_MODEL), f32),
        scratch_shapes=[pltpu.VMEM((tm + 2 * FFN_HALO, D_MODEL), bf16), pltpu.VMEM((tm, D_MODEL), f32)],
        compiler_params=_params("parallel", "arbitrary"),
        name="conv_ffn",
    )(a, a, a, h, mod, w_up, w_up, conv_w, conv_w, conv_b, conv_b, w_down)


def _final_norm_kernel(h_ref, g_ref, o_ref):
    h = h_ref[...]
    o_ref[...] = h * lax.rsqrt(jnp.mean(h * h, axis=-1, keepdims=True) + EPS) * g_ref[...]


def final_norm(h, g):
    L = h.shape[0]
    tm = _row_tile(L)
    return pl.pallas_call(
        _final_norm_kernel,
        grid=(L // tm,),
        in_specs=[pl.BlockSpec((tm, D_MODEL), lambda i: (i, 0)), pl.BlockSpec((1, D_MODEL), lambda i: (0, 0))],
        out_specs=pl.BlockSpec((tm, D_MODEL), lambda i: (i, 0)),
        out_shape=jax.ShapeDtypeStruct((L, D_MODEL), f32),
        compiler_params=_params("parallel"),
        name="final_norm",
    )(h, g)


def kernel(x, c, ctx, c_ctx, w_mod, b_mod, norm1_g, w_in, pool_w, pool_scale, na_rpb, ret_decay_fwd, ret_decay_bwd,
           ret_gn_g, w_out, norm2_g, w_up, conv_w, conv_b, w_down, final_g):
    B, L, _ = x.shape
    assert B == 1
    n_ctx = ctx.shape[1]
    h = x[0]
    hc = ctx[0]
    mods = mod_vectors(c, c_ctx, w_mod, b_mod)
    cos, sin = rope_tables(L)
    cos_id = jnp.ones((n_ctx, LANES), f32)
    sin_id = jnp.zeros((n_ctx, LANES), f32)
    zero_state = jnp.zeros((N_PAIRS, LANES, LANES), f32)
    for l in range(DEPTH):
        need_ctx = l < DEPTH - 1
        mx = mods[l, 0].reshape(6, D_MODEL)
        mc = mods[l, 1].reshape(6, D_MODEL)
        g1 = norm1_g[l].reshape(1, D_MODEL)
        g2 = norm2_g[l].reshape(1, D_MODEL)
        wi = w_in[l].astype(bf16)
        wo = w_out[l].astype(bf16)
        wu = w_up[l].astype(bf16)
        wd = w_down[l].astype(bf16)
        cb = conv_b[l].reshape(1, 2 * D_FF)
        wp = pool_blockdiag(pool_w[l])
        ps = pool_scale[l].reshape(1, POOL_DIM)
        bias = na_bias_table(na_rpb[l])
        lgf = ret_decay_fwd[l].astype(f32)
        lgb = ret_decay_bwd[l].astype(f32)
        lgf_lane = jnp.repeat(lgf, HEAD_DIM).reshape(1, RET_DIM)
        lgb_lane = jnp.repeat(lgb, HEAD_DIM).reshape(1, RET_DIM)
        gn = ret_gn_g[l].reshape(1, RET_DIM)

        pc, nac, retc, gatec = in_proj(hc, g1, mc, wi, cos_id, sin_id)
        px, nax, retx, gatex = in_proj(h, g1, mx, wi, cos, sin)

        sfc, sbc, sf0, sb0 = retention_states(retc, lgf_lane, lgb_lane, zero_state, zero_state)
        sfx, sbx, _, _ = retention_states(retx, lgf_lane, lgb_lane, sf0, sb0)

        pool_x = pool_mixer(px, wp, ps)
        na_x = neighborhood_attention(nax, nac, bias)
        ret_x = retention_outputs(retx, gatex, sfx, sbx, lgf, lgb, lgf_lane, lgb_lane, gn)
        h, a2 = out_proj(h, pool_x, na_x, ret_x, wo, mx, g2)
        h = conv_ffn(a2, h, mx, wu, conv_w[l], cb, wd)

        if need_ctx:
            pool_c = pool_mixer(pc, wp, ps)
            na_c = context_attention(nac)
            ret_c = retention_outputs(retc, gatec, sfc, sbc, lgf, lgb, lgf_lane, lgb_lane, gn)
            hc, a2c = out_proj(hc, pool_c, na_c, ret_c, wo, mc, g2)
            hc = conv_ffn(a2c, hc, mc, wu, conv_w[l], cb, wd)
    return final_norm(h, final_g.reshape(1, D_MODEL))[None]
```

```python
import functools

import numpy as np
import jax
import jax.numpy as jnp
from jax import lax
from jax.experimental import pallas as pl
from jax.experimental.pallas import tpu as pltpu

f32 = jnp.float32
bf16 = jnp.bfloat16

D_MODEL = 1024
DEPTH = 4
GRID_W = 64
HEAD_DIM = 64
POOL_GROUPS = 4
POOL_WINDOWS = (2, 4, 8, 16)
POOL_DIM = 256
NA_HEADS = 6
NA_DIM = 384
NA_KH = 8
NA_KW = 16
RET_HEADS = 6
RET_DIM = 384
RET_CHUNK = 128
D_FF = 2816
ROPE_BASE = 10000.0
ROT_QUARTER = HEAD_DIM // 4
EPS = 1e-6
NEG_INF = -1e30
IN_DIM = POOL_DIM + 3 * NA_DIM + 4 * RET_DIM
OFF_NA = POOL_DIM
OFF_RET = OFF_NA + 3 * NA_DIM
OFF_GATE = OFF_RET + 3 * RET_DIM

LANES = 128
N_PAIRS = NA_HEADS // 2
VMEM_LIMIT = 56 * 1024 * 1024

NT_DIMS = (((1,), (1,)), ((), ()))
TN_DIMS = (((0,), (0,)), ((), ()))


def _params(*sem):
    return pltpu.CompilerParams(dimension_semantics=sem, vmem_limit_bytes=VMEM_LIMIT)


def _row_tile(L):
    return min(L, 512)


def _silu(x):
    return x * jax.nn.sigmoid(x)


def _lo_half():
    return lax.broadcasted_iota(jnp.int32, (1, LANES), 1) < HEAD_DIM


def _mod_kernel(s_ref, w_ref, b_ref, o_ref):
    a = _silu(s_ref[...]).astype(bf16)
    o_ref[...] = jnp.dot(a, w_ref[...].astype(bf16), preferred_element_type=f32) + b_ref[...]


def mod_vectors(c, c_ctx, w_mod, b_mod):
    s = jnp.zeros((8, D_MODEL), f32).at[0].set(c[0]).at[1].set(c_ctx)
    return pl.pallas_call(
        _mod_kernel,
        grid=(DEPTH, 6),
        in_specs=[
            pl.BlockSpec((8, D_MODEL), lambda l, j: (0, 0)),
            pl.BlockSpec((None, D_MODEL, D_MODEL), lambda l, j: (l, 0, j)),
            pl.BlockSpec((None, 1, D_MODEL), lambda l, j: (l, 0, j)),
        ],
        out_specs=pl.BlockSpec((None, 8, D_MODEL), lambda l, j: (l, 0, j)),
        out_shape=jax.ShapeDtypeStruct((DEPTH, 8, 6 * D_MODEL), f32),
        compiler_params=_params("parallel", "parallel"),
        name="mod_vectors",
    )(s, w_mod, b_mod.reshape(DEPTH, 1, 6 * D_MODEL))


def _in_proj_kernel(h_ref, g_ref, mod_ref, w_ref, cos_ref, sin_ref, p_ref, na_ref, ret_ref, gate_ref):
    h = h_ref[...]
    y = h * lax.rsqrt(jnp.mean(h * h, axis=-1, keepdims=True) + EPS) * g_ref[...]
    a = (y * (1.0 + mod_ref[1:2, :]) + mod_ref[0:1, :]).astype(bf16)

    def proj(lo, n):
        return jnp.dot(a, w_ref[:, lo:lo + n], preferred_element_type=f32)

    p_ref[...] = proj(0, POOL_DIM)
    na = proj(OFF_NA, 3 * NA_DIM)
    na_ref[:, 0:NA_DIM] = (na[:, 0:NA_DIM] * HEAD_DIM ** -0.5).astype(bf16)
    na_ref[:, NA_DIM:] = na[:, NA_DIM:].astype(bf16)
    first = lax.broadcasted_iota(jnp.int32, (1, LANES), 1) % (2 * ROT_QUARTER) < ROT_QUARTER
    cos = cos_ref[...]
    sin = sin_ref[...]
    qk = proj(OFF_RET, 2 * RET_DIM)
    for slab in range(2 * RET_DIM // LANES):
        lo = LANES * slab
        x = qk[:, lo:lo + LANES]
        partner = jnp.where(first, pltpu.roll(x, LANES - ROT_QUARTER, 1), pltpu.roll(x, ROT_QUARTER, 1))
        x = x * cos + partner * sin
        if lo >= RET_DIM:
            x = x * HEAD_DIM ** -0.5
        ret_ref[:, lo:lo + LANES] = x.astype(bf16)
    vg = proj(OFF_RET + 2 * RET_DIM, 2 * RET_DIM)
    ret_ref[:, 2 * RET_DIM:] = vg[:, :RET_DIM].astype(bf16)
    gate_ref[...] = vg[:, RET_DIM:]


def in_proj(h, g, mod, w_in, cos, sin):
    L = h.shape[0]
    tm = _row_tile(L)
    row = lambda n: pl.BlockSpec((tm, n), lambda i: (i, 0))
    full = lambda a: pl.BlockSpec(a.shape, lambda i: (0,) * a.ndim)
    return pl.pallas_call(
        _in_proj_kernel,
        grid=(L // tm,),
        in_specs=[row(D_MODEL), full(g), full(mod), full(w_in), row(LANES), row(LANES)],
        out_specs=[row(POOL_DIM), row(3 * NA_DIM), row(3 * RET_DIM), row(RET_DIM)],
        out_shape=[
            jax.ShapeDtypeStruct((L, POOL_DIM), f32),
            jax.ShapeDtypeStruct((L, 3 * NA_DIM), bf16),
            jax.ShapeDtypeStruct((L, 3 * RET_DIM), bf16),
            jax.ShapeDtypeStruct((L, RET_DIM), f32),
        ],
        compiler_params=_params("parallel"),
        name="in_proj",
    )(h, g, mod, w_in, cos, sin)


def rope_tables(L):
    pos = jnp.arange(L)
    inv = ROPE_BASE ** (-jnp.arange(ROT_QUARTER, dtype=f32) / ROT_QUARTER)
    ang_r = (pos // GRID_W).astype(f32)[:, None] * inv[None, :]
    ang_c = (pos % GRID_W).astype(f32)[:, None] * inv[None, :]
    cos = jnp.concatenate([jnp.cos(ang_r)] * 2 + [jnp.cos(ang_c)] * 2, axis=-1)
    sin = jnp.concatenate([-jnp.sin(ang_r), jnp.sin(ang_r), -jnp.sin(ang_c), jnp.sin(ang_c)], axis=-1)
    return jnp.tile(cos, (1, 2)), jnp.tile(sin, (1, 2))


POOL_HALO = 8


def _pool_kernel(prev_ref, x_ref, next_ref, w_ref, scale_ref, o_ref, ext_ref, *, tm, L):
    i = pl.program_id(0)
    n = pl.num_programs(0)
    ext_ref[0:POOL_HALO, :] = jnp.where(i > 0, prev_ref[...], 0.0)
    ext_ref[POOL_HALO:POOL_HALO + tm, :] = x_ref[...]
    ext_ref[POOL_HALO + tm:, :] = jnp.where(i < n - 1, next_ref[...], 0.0)

    def sh(d):
        return ext_ref[POOL_HALO + d:POOL_HALO + d + tm, :]

    x = x_ref[...]
    sums = []
    acc = None
    prev_w = 0
    for w in POOL_WINDOWS:
        for d in list(range(-(w // 2), -(prev_w // 2))) + list(range(prev_w // 2, w // 2)):
            term = x if d == 0 else sh(d)
            acc = term if acc is None else acc + term
        sums.append(acc)
        prev_w = w
    lane = lax.broadcasted_iota(jnp.int32, (1, POOL_DIM), 1)
    t = i * tm + lax.broadcasted_iota(jnp.int32, (tm, 1), 0)
    gdim = POOL_DIM // POOL_GROUPS
    total = sums[-1]
    count = None
    for gi in range(POOL_GROUPS - 1, -1, -1):
        w = POOL_WINDOWS[gi]
        c = (jnp.minimum(t - w // 2 + w, L) - jnp.maximum(t - w // 2, 0)).astype(f32)
        if count is None:
            count = jnp.broadcast_to(c, (tm, POOL_DIM))
        else:
            sel = lane < gdim * (gi + 1)
            total = jnp.where(sel, sums[gi], total)
            count = jnp.where(sel, c, count)
    m = (total / count - x).astype(bf16)
    y = jnp.dot(m, w_ref[...], preferred_element_type=f32) * scale_ref[...]
    o_ref[...] = y.astype(bf16)


def pool_mixer(p, w_blockdiag, scale):
    L = p.shape[0]
    tm = _row_tile(L)
    hb = tm // POOL_HALO
    return pl.pallas_call(
        functools.partial(_pool_kernel, tm=tm, L=L),
        grid=(L // tm,),
        in_specs=[
            pl.BlockSpec((POOL_HALO, POOL_DIM), lambda i: (jnp.maximum(i * hb - 1, 0), 0)),
            pl.BlockSpec((tm, POOL_DIM), lambda i: (i, 0)),
            pl.BlockSpec((POOL_HALO, POOL_DIM), lambda i: (jnp.minimum((i + 1) * hb, L // POOL_HALO - 1), 0)),
            pl.BlockSpec((POOL_DIM, POOL_DIM), lambda i: (0, 0)),
            pl.BlockSpec((1, POOL_DIM), lambda i: (0, 0)),
        ],
        out_specs=pl.BlockSpec((tm, POOL_DIM), lambda i: (i, 0)),
        out_shape=jax.ShapeDtypeStruct((L, POOL_DIM), bf16),
        scratch_shapes=[pltpu.VMEM((tm + 2 * POOL_HALO, POOL_DIM), f32)],
        compiler_params=_params("parallel"),
        name="pool_mixer",
    )(p, p, p, w_blockdiag, scale)


def pool_blockdiag(pool_w):
    gdim = POOL_DIM // POOL_GROUPS
    w = jnp.zeros((POOL_DIM, POOL_DIM), f32)
    for g in range(POOL_GROUPS):
        w = w.at[g * gdim:(g + 1) * gdim, g * gdim:(g + 1) * gdim].set(pool_w[g])
    return w.astype(bf16)


NA_RB = 4
NA_QT = NA_RB * GRID_W
NA_WIN = NA_RB + NA_KH
NA_KB = NA_WIN // NA_RB
NA_DR_MASKED = 2 * NA_KH - 1
NA_VARIANTS = 3


def na_bias_tiles(rpb):
    qcol = np.arange(GRID_W)[:, None]
    kcol = np.arange(GRID_W)[None, :]
    ws = np.clip(qcol - NA_KW // 2, 0, GRID_W - NA_KW)
    valid = (kcol >= ws) & (kcol < ws + NA_KW)
    dc = np.clip(kcol - qcol + NA_KW - 1, 0, 2 * NA_KW - 2)
    t = jnp.where(jnp.asarray(valid)[None, None], rpb.astype(f32)[:, :, dc], NEG_INF)
    return jnp.concatenate([t, jnp.full((NA_HEADS, 1, GRID_W, GRID_W), NEG_INF, f32)], axis=1)


def _na_key_base(g, nb):
    return jnp.clip(g - 1, 0, nb - NA_KB)


def _na_dr_maps(rows):
    nb = rows // NA_RB
    maps = []
    for g in range(nb):
        kb = NA_RB * min(max(g - 1, 0), nb - NA_KB)
        m = np.full((NA_RB, NA_WIN), NA_DR_MASKED)
        for j in range(NA_RB):
            r = NA_RB * g + j
            rs = min(max(r - NA_KH // 2, 0), rows - NA_KH)
            for i in range(rs - kb, rs - kb + NA_KH):
                m[j, i] = kb + i - r + NA_KH - 1
        maps.append(m)
    variants = [maps[0], maps[1], maps[nb - 1]]
    for g in range(nb):
        assert (maps[g] == variants[0 if g == 0 else 2 if g == nb - 1 else 1]).all()
    return variants


def _softmax_pv(s_parts, v_parts):
    m = functools.reduce(jnp.maximum, [jnp.max(s, axis=-1, keepdims=True) for s in s_parts])
    den = None
    out = None
    for s, v in zip(s_parts, v_parts):
        p = jnp.exp(s - m)
        ps = jnp.sum(p, axis=-1, keepdims=True)
        den = ps if den is None else den + ps
        pv = jnp.dot(p.astype(bf16), v, preferred_element_type=f32)
        out = pv if out is None else out + pv
    return out / den


def _na_kernel(q_ref, k0_ref, k1_ref, k2_ref, v0_ref, v1_ref, v2_ref, kc_ref, vc_ref, tiles_ref, o_ref,
               bias_scr, *, nb, dr_maps):
    g = pl.program_id(0)

    @pl.when(g == 0)
    def _():
        for var in range(NA_VARIANTS):
            for j in range(NA_RB):
                for i in range(NA_WIN):
                    bias_scr[var, :, GRID_W * j:GRID_W * (j + 1), GRID_W * i:GRID_W * (i + 1)] = (
                        tiles_ref[:, int(dr_maps[var][j, i])])

    var = jnp.where(g == 0, 0, jnp.where(g == nb - 1, 2, 1))
    lo_half = _lo_half()
    k_refs = (k0_ref, k1_ref, k2_ref)
    v_refs = (v0_ref, v1_ref, v2_ref)
    for hp in range(N_PAIRS):
        sl = slice(LANES * hp, LANES * (hp + 1))
        q2 = q_ref[:, sl]
        values = [vr[:, sl] for vr in v_refs] + [vc_ref[:, sl]]
        outs = []
        for sub in range(2):
            qm = jnp.where(lo_half if sub == 0 else jnp.logical_not(lo_half), q2, jnp.zeros_like(q2))
            scores = []
            for kb in range(NA_KB):
                s = lax.dot_general(qm, k_refs[kb][:, sl], NT_DIMS, preferred_element_type=f32)
                tb = bias_scr[var, 2 * hp + sub, :, NA_QT * kb:NA_QT * (kb + 1)]
                scores.append(jnp.where(tb < 0.5 * NEG_INF, NEG_INF, s + tb))
            scores.append(lax.dot_general(qm, kc_ref[:, sl], NT_DIMS, preferred_element_type=f32))
            outs.append(_softmax_pv(scores, values))
        o_ref[:, sl] = jnp.where(lo_half, outs[0], outs[1]).astype(bf16)


def neighborhood_attention(na, na_ctx, tiles):
    L = na.shape[0]
    rows = L // GRID_W
    nb = rows // NA_RB
    assert rows % NA_RB == 0 and nb > NA_KB
    ctx = na_ctx.shape[0]
    blk = lambda off, col: pl.BlockSpec((NA_QT, NA_DIM), lambda g: (_na_key_base(g, nb) + off, col))
    return pl.pallas_call(
        functools.partial(_na_kernel, nb=nb, dr_maps=_na_dr_maps(rows)),
        grid=(nb,),
        in_specs=[
            pl.BlockSpec((NA_QT, NA_DIM), lambda g: (g, 0)),
            blk(0, 1), blk(1, 1), blk(2, 1),
            blk(0, 2), blk(1, 2), blk(2, 2),
            pl.BlockSpec((ctx, NA_DIM), lambda g: (0, 1)),
            pl.BlockSpec((ctx, NA_DIM), lambda g: (0, 2)),
            pl.BlockSpec(tiles.shape, lambda g: (0, 0, 0, 0)),
        ],
        out_specs=pl.BlockSpec((NA_QT, NA_DIM), lambda g: (g, 0)),
        out_shape=jax.ShapeDtypeStruct((L, NA_DIM), bf16),
        scratch_shapes=[pltpu.VMEM((NA_VARIANTS, NA_HEADS, NA_QT, NA_WIN * GRID_W), f32)],
        compiler_params=_params("arbitrary"),
        name="neighborhood_attention",
    )(na, na, na, na, na, na, na, na_ctx, na_ctx, tiles)


def _ctx_attn_kernel(q_ref, k_ref, v_ref, o_ref):
    lo_half = _lo_half()
    for hp in range(N_PAIRS):
        sl = slice(LANES * hp, LANES * (hp + 1))
        q2 = q_ref[:, sl]
        k2 = k_ref[:, sl]
        v2 = v_ref[:, sl]
        outs = []
        for sub in range(2):
            qm = jnp.where(lo_half if sub == 0 else jnp.logical_not(lo_half), q2, jnp.zeros_like(q2))
            s = lax.dot_general(qm, k2, NT_DIMS, preferred_element_type=f32)
            outs.append(_softmax_pv([s], [v2]))
        o_ref[:, sl] = jnp.where(lo_half, outs[0], outs[1]).astype(bf16)


def context_attention(na_ctx):
    ctx = na_ctx.shape[0]
    col = lambda c: pl.BlockSpec((ctx, NA_DIM), lambda i: (0, c))
    return pl.pallas_call(
        _ctx_attn_kernel,
        grid=(1,),
        in_specs=[col(0), col(1), col(2)],
        out_specs=pl.BlockSpec((ctx, NA_DIM), lambda i: (0, 0)),
        out_shape=jax.ShapeDtypeStruct((ctx, NA_DIM), bf16),
        compiler_params=_params("arbitrary"),
        name="context_attention",
    )(na_ctx, na_ctx, na_ctx)


RET_STEP = 4


def _pair_blockmask():
    r = lax.broadcasted_iota(jnp.int32, (LANES, LANES), 0) < HEAD_DIM
    c = lax.broadcasted_iota(jnp.int32, (LANES, LANES), 1) < HEAD_DIM
    return r == c


def _ret_state_kernel(kf_ref, vf_ref, kb_ref, vb_ref, lgf_ref, lgb_ref, s0f_ref, s0b_ref,
                      sf_out, sb_out, sf_fin, sb_fin, sf_scr, sb_scr, *, cs):
    n = pl.program_id(0)

    @pl.when(n == 0)
    def _():
        sf_scr[...] = s0f_ref[...]
        sb_scr[...] = s0b_ref[...]

    c = lax.broadcasted_iota(jnp.int32, (RET_CHUNK, 1), 0).astype(f32)
    blockmask = _pair_blockmask()
    dirs = (
        (kf_ref, vf_ref, lgf_ref, sf_scr, sf_out, sf_fin, RET_CHUNK - 1.0 - c, range(cs)),
        (kb_ref, vb_ref, lgb_ref, sb_scr, sb_out, sb_fin, c, range(cs - 1, -1, -1)),
    )
    for k_ref, v_ref, lg_ref, scr, out, fin, steps, order in dirs:
        lg = lg_ref[...]
        kdec = jnp.exp(steps * lg)
        cdec = jnp.exp(float(RET_CHUNK) * lg)
        state = [scr[hp] for hp in range(N_PAIRS)]
        for ci in order:
            rows = slice(RET_CHUNK * ci, RET_CHUNK * (ci + 1))
            kd = (k_ref[rows, :].astype(f32) * kdec).astype(bf16)
            v = v_ref[rows, :]
            for hp in range(N_PAIRS):
                sl = slice(LANES * hp, LANES * (hp + 1))
                out[ci, hp] = state[hp]
                kv = lax.dot_general(kd[:, sl], v[:, sl], TN_DIMS, preferred_element_type=f32)
                state[hp] = state[hp] * cdec[:, sl] + jnp.where(blockmask, kv, 0.0)
        for hp in range(N_PAIRS):
            scr[hp] = state[hp]
            fin[hp] = state[hp]


def retention_states(ret, lgf_lane, lgb_lane, s0f, s0b):
    L = ret.shape[0]
    N = L // RET_CHUNK
    cs = min(RET_STEP, N)
    ns = N // cs
    blk = lambda f, col: pl.BlockSpec((cs * RET_CHUNK, RET_DIM), lambda n: (f(n), col))
    fwd = lambda n: n
    bwd = lambda n: ns - 1 - n
    vec = pl.BlockSpec((1, RET_DIM), lambda n: (0, 0))
    st = pl.BlockSpec((N_PAIRS, LANES, LANES), lambda n: (0, 0, 0))
    st_shape = jax.ShapeDtypeStruct((N_PAIRS, LANES, LANES), f32)
    all_shape = jax.ShapeDtypeStruct((N, N_PAIRS, LANES, LANES), f32)
    return pl.pallas_call(
        functools.partial(_ret_state_kernel, cs=cs),
        grid=(ns,),
        in_specs=[blk(fwd, 1), blk(fwd, 2), blk(bwd, 1), blk(bwd, 2), vec, vec, st, st],
        out_specs=[
            pl.BlockSpec((cs, N_PAIRS, LANES, LANES), lambda n: (n, 0, 0, 0)),
            pl.BlockSpec((cs, N_PAIRS, LANES, LANES), lambda n: (ns - 1 - n, 0, 0, 0)),
            st, st,
        ],
        out_shape=[all_shape, all_shape, st_shape, st_shape],
        scratch_shapes=[pltpu.VMEM((N_PAIRS, LANES, LANES), f32), pltpu.VMEM((N_PAIRS, LANES, LANES), f32)],
        compiler_params=_params("arbitrary"),
        name="retention_states",
    )(ret, ret, ret, ret, lgf_lane, lgb_lane, s0f, s0b)


def _ret_out_kernel(lgf_s, lgb_s, q_ref, k_ref, v_ref, gate_ref, sf_ref, sb_ref, lgf_ref, lgb_ref, gn_ref, o_ref,
                    *, cs):
    lo_half = _lo_half()
    diff = (lax.broadcasted_iota(jnp.int32, (RET_CHUNK, RET_CHUNK), 0)
            - lax.broadcasted_iota(jnp.int32, (RET_CHUNK, RET_CHUNK), 1)).astype(f32)
    c = lax.broadcasted_iota(jnp.int32, (RET_CHUNK, 1), 0).astype(f32)
    qdec_f = jnp.exp((c + 1.0) * lgf_ref[...])
    qdec_b = jnp.exp((RET_CHUNK - c) * lgb_ref[...])
    decay = [jnp.where(diff >= 0, jnp.exp(diff * lgf_s[h]), 0.0) + jnp.where(diff <= 0, jnp.exp(-diff * lgb_s[h]), 0.0)
             for h in range(RET_HEADS)]
    inv_hd = 1.0 / HEAD_DIM

    def head_mean(t):
        m0 = jnp.sum(jnp.where(lo_half, t, 0.0), axis=-1, keepdims=True) * inv_hd
        m1 = jnp.sum(jnp.where(lo_half, 0.0, t), axis=-1, keepdims=True) * inv_hd
        return jnp.where(lo_half, m0, m1)

    for ci in range(cs):
        rows = slice(RET_CHUNK * ci, RET_CHUNK * (ci + 1))
        q = q_ref[rows, :]
        qf32 = q.astype(f32)
        q_f = (qf32 * qdec_f).astype(bf16)
        q_b = (qf32 * qdec_b).astype(bf16)
        for hp in range(N_PAIRS):
            sl = slice(LANES * hp, LANES * (hp + 1))
            q2 = q[:, sl]
            k2 = k_ref[rows, sl]
            v2 = v_ref[rows, sl]
            lhs = []
            rhs = []
            for sub in range(2):
                hmask = lo_half if sub == 0 else jnp.logical_not(lo_half)
                qm = jnp.where(hmask, q2, jnp.zeros_like(q2))
                a = lax.dot_general(qm, k2, NT_DIMS, preferred_element_type=f32)
                lhs.append((a * decay[2 * hp + sub]).astype(bf16))
                rhs.append(jnp.where(hmask, v2, jnp.zeros_like(v2)))
            lhs += [q_f[:, sl], q_b[:, sl]]
            rhs += [sf_ref[ci, hp].astype(bf16), sb_ref[ci, hp].astype(bf16)]
            y = jnp.dot(jnp.concatenate(lhs, axis=1), jnp.concatenate(rhs, axis=0), preferred_element_type=f32)
            d = y - head_mean(y)
            yn = d * lax.rsqrt(head_mean(d * d) + EPS) * gn_ref[:, sl]
            o_ref[rows, sl] = (_silu(gate_ref[rows, sl]) * yn).astype(bf16)


def retention_outputs(ret, gate, sf, sb, lgf, lgb, lgf_lane, lgb_lane, gn):
    L = ret.shape[0]
    N = L // RET_CHUNK
    cs = min(RET_STEP, N)
    col = lambda c: pl.BlockSpec((cs * RET_CHUNK, RET_DIM), lambda n, *_: (n, c))
    vec = pl.BlockSpec((1, RET_DIM), lambda n, *_: (0, 0))
    st = pl.BlockSpec((cs, N_PAIRS, LANES, LANES), lambda n, *_: (n, 0, 0, 0))
    return pl.pallas_call(
        functools.partial(_ret_out_kernel, cs=cs),
        grid_spec=pltpu.PrefetchScalarGridSpec(
            num_scalar_prefetch=2,
            grid=(N // cs,),
            in_specs=[col(0), col(1), col(2), col(0), st, st, vec, vec, vec],
            out_specs=pl.BlockSpec((cs * RET_CHUNK, RET_DIM), lambda n, *_: (n, 0)),
        ),
        out_shape=jax.ShapeDtypeStruct((L, RET_DIM), bf16),
        compiler_params=_params("parallel"),
        name="retention_outputs",
    )(lgf, lgb, ret, ret, ret, gate, sf, sb, lgf_lane, lgb_lane, gn)


FFN_HALO = 16
FFN_TF = 256


def _out_ffn_kernel(hp_ref, h_ref, hn_ref, pp_ref, p_ref, pn_ref, np_ref, n_ref, nn_ref, rp_ref, r_ref, rn_ref,
                    mod_ref, g_ref, wo_ref, wu_ref, cw_ref, cb_ref, wd_ref, fg_ref, o_ref, aext, gbuf, *, tm, final):
    i = pl.program_id(0)
    ni = pl.num_programs(0)
    ext = tm + 2 * FFN_HALO

    def rows3(prev, cur, nxt):
        return jnp.concatenate([prev[...], cur[...], nxt[...]], axis=0)

    mixed = jnp.concatenate([rows3(pp_ref, p_ref, pn_ref), rows3(np_ref, n_ref, nn_ref),
                             rows3(rp_ref, r_ref, rn_ref)], axis=1)
    hx = rows3(hp_ref, h_ref, hn_ref) + mod_ref[2:3, :] * jnp.dot(mixed, wo_ref[...], preferred_element_type=f32)
    o_ref[...] = hx[FFN_HALO:FFN_HALO + tm]
    a = hx * lax.rsqrt(jnp.mean(hx * hx, axis=-1, keepdims=True) + EPS) * g_ref[...]
    a = a * (1.0 + mod_ref[4:5, :]) + mod_ref[3:4, :]
    row = lax.broadcasted_iota(jnp.int32, (ext, 1), 0)
    inside = jnp.logical_and(jnp.logical_or(row >= FFN_HALO, i > 0), jnp.logical_or(row < FFN_HALO + tm, i < ni - 1))
    aext[...] = jnp.where(inside, a, 0.0).astype(bf16)
    ae = aext[...]

    def branch(lo):
        u = jnp.dot(ae, wu_ref[:, lo:lo + FFN_TF], preferred_element_type=f32)
        up = pltpu.roll(u, 1, 0)[FFN_HALO:FFN_HALO + tm]
        un = pltpu.roll(u, ext - 1, 0)[FFN_HALO:FFN_HALO + tm]
        uc = u[FFN_HALO:FFN_HALO + tm]
        cw = cw_ref[:, lo:lo + FFN_TF]
        return up * cw[0:1, :] + uc * cw[1:2, :] + un * cw[2:3, :] + cb_ref[:, lo:lo + FFN_TF]

    for s in range(D_FF // FFN_TF):
        lo = FFN_TF * s
        gbuf[:, lo:lo + FFN_TF] = (_silu(branch(lo)) * branch(D_FF + lo)).astype(bf16)
    y = jnp.dot(gbuf[...], wd_ref[...], preferred_element_type=f32)
    out = o_ref[...] + mod_ref[5:6, :] * y
    if final:
        out = out * lax.rsqrt(jnp.mean(out * out, axis=-1, keepdims=True) + EPS) * fg_ref[...]
    o_ref[...] = out


def out_proj_ffn(h, pool_x, na_x, ret_x, mod, g2, w_out, w_up, conv_w, conv_b, w_down, final_g, final):
    L = h.shape[0]
    tm = _row_tile(L)
    hb = tm // FFN_HALO

    def halo3(n):
        return [
            pl.BlockSpec((FFN_HALO, n), lambda i: (jnp.maximum(i * hb - 1, 0), 0)),
            pl.BlockSpec((tm, n), lambda i: (i, 0)),
            pl.BlockSpec((FFN_HALO, n), lambda i: (jnp.minimum((i + 1) * hb, L // FFN_HALO - 1), 0)),
        ]

    const = lambda arr: pl.BlockSpec(arr.shape, lambda i: (0,) * arr.ndim, pipeline_mode=pl.Buffered(1))
    consts = (mod, g2, w_out, w_up, conv_w, conv_b, w_down, final_g)
    return pl.pallas_call(
        functools.partial(_out_ffn_kernel, tm=tm, final=final),
        grid=(L // tm,),
        in_specs=halo3(D_MODEL) + halo3(POOL_DIM) + halo3(NA_DIM) + halo3(RET_DIM) + [const(a) for a in consts],
        out_specs=pl.BlockSpec((tm, D_MODEL), lambda i: (i, 0)),
        out_shape=jax.ShapeDtypeStruct((L, D_MODEL), f32),
        scratch_shapes=[pltpu.VMEM((tm + 2 * FFN_HALO, D_MODEL), bf16), pltpu.VMEM((tm, D_FF), bf16)],
        compiler_params=_params("parallel"),
        name="out_proj_ffn",
    )(h, h, h, pool_x, pool_x, pool_x, na_x, na_x, na_x, ret_x, ret_x, ret_x, *consts)


def kernel(x, c, ctx, c_ctx, w_mod, b_mod, norm1_g, w_in, pool_w, pool_scale, na_rpb, ret_decay_fwd, ret_decay_bwd,
           ret_gn_g, w_out, norm2_g, w_up, conv_w, conv_b, w_down, final_g):
    B, L, _ = x.shape
    assert B == 1
    n_ctx = ctx.shape[1]
    h = x[0]
    hc = ctx[0]
    mods = mod_vectors(c, c_ctx, w_mod, b_mod)
    cos, sin = rope_tables(L)
    cos_id = jnp.ones((n_ctx, LANES), f32)
    sin_id = jnp.zeros((n_ctx, LANES), f32)
    zero_state = jnp.zeros((N_PAIRS, LANES, LANES), f32)
    fg = final_g.reshape(1, D_MODEL)
    for l in range(DEPTH):
        need_ctx = l < DEPTH - 1
        mx = mods[l, 0].reshape(6, D_MODEL)
        mc = mods[l, 1].reshape(6, D_MODEL)
        g1 = norm1_g[l].reshape(1, D_MODEL)
        g2 = norm2_g[l].reshape(1, D_MODEL)
        wi = w_in[l].astype(bf16)
        wo = w_out[l].astype(bf16)
        wu = w_up[l].astype(bf16)
        wd = w_down[l].astype(bf16)
        cb = conv_b[l].reshape(1, 2 * D_FF)
        wp = pool_blockdiag(pool_w[l])
        ps = pool_scale[l].reshape(1, POOL_DIM)
        bias = na_bias_tiles(na_rpb[l])
        lgf = ret_decay_fwd[l].astype(f32)
        lgb = ret_decay_bwd[l].astype(f32)
        lgf_lane = jnp.repeat(lgf, HEAD_DIM).reshape(1, RET_DIM)
        lgb_lane = jnp.repeat(lgb, HEAD_DIM).reshape(1, RET_DIM)
        gn = ret_gn_g[l].reshape(1, RET_DIM)

        pc, nac, retc, gatec = in_proj(hc, g1, mc, wi, cos_id, sin_id)
        px, nax, retx, gatex = in_proj(h, g1, mx, wi, cos, sin)

        sfc, sbc, sf0, sb0 = retention_states(retc, lgf_lane, lgb_lane, zero_state, zero_state)
        sfx, sbx, _, _ = retention_states(retx, lgf_lane, lgb_lane, sf0, sb0)

        pool_x = pool_mixer(px, wp, ps)
        na_x = neighborhood_attention(nax, nac, bias)
        ret_x = retention_outputs(retx, gatex, sfx, sbx, lgf, lgb, lgf_lane, lgb_lane, gn)
        h = out_proj_ffn(h, pool_x, na_x, ret_x, mx, g2, wo, wu, conv_w[l], cb, wd, fg, final=not need_ctx)

        if need_ctx:
            pool_c = pool_mixer(pc, wp, ps)
            na_c = context_attention(nac)
            ret_c = retention_outputs(retc, gatec, sfc, sbc, lgf, lgb, lgf_lane, lgb_lane, gn)
            hc = out_proj_ffn(hc, pool_c, na_c, ret_c, mc, g2, wo, wu, conv_w[l], cb, wd, fg, final=False)
    return h[None]
```

```python
import functools

import numpy as np
import jax
import jax.numpy as jnp
from jax import lax
from jax.experimental import pallas as pl
from jax.experimental.pallas import tpu as pltpu

f32 = jnp.float32
bf16 = jnp.bfloat16

D_MODEL = 1024
DEPTH = 4
GRID_W = 64
HEAD_DIM = 64
POOL_GROUPS = 4
POOL_WINDOWS = (2, 4, 8, 16)
POOL_DIM = 256
NA_HEADS = 6
NA_DIM = 384
NA_KH = 8
NA_KW = 16
RET_HEADS = 6
RET_DIM = 384
RET_CHUNK = 128
D_FF = 2816
ROPE_BASE = 10000.0
ROT_QUARTER = HEAD_DIM // 4
EPS = 1e-6
NEG_INF = -1e30
IN_DIM = POOL_DIM + 3 * NA_DIM + 4 * RET_DIM
OFF_NA = POOL_DIM
OFF_RET = OFF_NA + 3 * NA_DIM
OFF_GATE = OFF_RET + 3 * RET_DIM

LANES = 128
N_PAIRS = NA_HEADS // 2
VMEM_LIMIT = 56 * 1024 * 1024

LOG2E = float(np.log2(np.e))
NT_DIMS = (((1,), (1,)), ((), ()))
TN_DIMS = (((0,), (0,)), ((), ()))


def _params(*sem):
    return pltpu.CompilerParams(dimension_semantics=sem, vmem_limit_bytes=VMEM_LIMIT)


def _layer_spec(stack, layer, **kw):
    return pl.BlockSpec((None,) + stack.shape[1:], lambda i: (layer, 0, 0), **kw)


def _row_tile(L):
    return min(L, 512)


def _silu(x):
    return x * jax.nn.sigmoid(x)


def _lo_half():
    return lax.broadcasted_iota(jnp.int32, (1, LANES), 1) < HEAD_DIM


def _mod_kernel(s_ref, w_ref, b_ref, o_ref):
    a = _silu(s_ref[...]).astype(bf16)
    o_ref[...] = jnp.dot(a, w_ref[...].astype(bf16), preferred_element_type=f32) + b_ref[...]


def mod_vectors(c, c_ctx, w_mod, b_mod):
    s = jnp.zeros((8, D_MODEL), f32).at[0].set(c[0]).at[1].set(c_ctx)
    return pl.pallas_call(
        _mod_kernel,
        grid=(DEPTH, 6),
        in_specs=[
            pl.BlockSpec((8, D_MODEL), lambda l, j: (0, 0)),
            pl.BlockSpec((None, D_MODEL, D_MODEL), lambda l, j: (l, 0, j)),
            pl.BlockSpec((None, 1, D_MODEL), lambda l, j: (l, 0, j)),
        ],
        out_specs=pl.BlockSpec((None, 8, D_MODEL), lambda l, j: (l, 0, j)),
        out_shape=jax.ShapeDtypeStruct((DEPTH, 8, 6 * D_MODEL), f32),
        compiler_params=_params("parallel", "parallel"),
        name="mod_vectors",
    )(s, w_mod, b_mod.reshape(DEPTH, 1, 6 * D_MODEL))


def _in_proj_kernel(h_ref, g_ref, mod_ref, w_ref, cos_ref, sin_ref, p_ref, na_ref, ret_ref, gate_ref):
    h = h_ref[...]
    y = h * lax.rsqrt(jnp.mean(h * h, axis=-1, keepdims=True) + EPS) * g_ref[...]
    a = (y * (1.0 + mod_ref[1:2, :]) + mod_ref[0:1, :]).astype(bf16)

    def proj(lo, n):
        return jnp.dot(a, w_ref[:, lo:lo + n], preferred_element_type=f32)

    p_ref[...] = proj(0, POOL_DIM)
    na = proj(OFF_NA, 3 * NA_DIM)
    na_ref[:, 0:NA_DIM] = (na[:, 0:NA_DIM] * (HEAD_DIM ** -0.5 * LOG2E)).astype(bf16)
    na_ref[:, NA_DIM:] = na[:, NA_DIM:].astype(bf16)
    first = lax.broadcasted_iota(jnp.int32, (1, LANES), 1) % (2 * ROT_QUARTER) < ROT_QUARTER
    cos = cos_ref[...]
    sin = sin_ref[...]
    qk = proj(OFF_RET, 2 * RET_DIM)
    for slab in range(2 * RET_DIM // LANES):
        lo = LANES * slab
        x = qk[:, lo:lo + LANES]
        partner = jnp.where(first, pltpu.roll(x, LANES - ROT_QUARTER, 1), pltpu.roll(x, ROT_QUARTER, 1))
        x = x * cos + partner * sin
        if lo >= RET_DIM:
            x = x * HEAD_DIM ** -0.5
        ret_ref[:, lo:lo + LANES] = x.astype(bf16)
    vg = proj(OFF_RET + 2 * RET_DIM, 2 * RET_DIM)
    ret_ref[:, 2 * RET_DIM:] = vg[:, :RET_DIM].astype(bf16)
    gate_ref[...] = vg[:, RET_DIM:]


def in_proj(h, g, mod, w_in, layer, cos, sin):
    L = h.shape[0]
    tm = _row_tile(L)
    row = lambda n: pl.BlockSpec((tm, n), lambda i: (i, 0))
    full = lambda a: pl.BlockSpec(a.shape, lambda i: (0,) * a.ndim)
    return pl.pallas_call(
        _in_proj_kernel,
        grid=(L // tm,),
        in_specs=[row(D_MODEL), full(g), full(mod), _layer_spec(w_in, layer), row(LANES), row(LANES)],
        out_specs=[row(POOL_DIM), row(3 * NA_DIM), row(3 * RET_DIM), row(RET_DIM)],
        out_shape=[
            jax.ShapeDtypeStruct((L, POOL_DIM), f32),
            jax.ShapeDtypeStruct((L, 3 * NA_DIM), bf16),
            jax.ShapeDtypeStruct((L, 3 * RET_DIM), bf16),
            jax.ShapeDtypeStruct((L, RET_DIM), f32),
        ],
        compiler_params=_params("parallel"),
        name="in_proj",
    )(h, g, mod, w_in, cos, sin)


def rope_tables(L):
    pos = jnp.arange(L)
    inv = ROPE_BASE ** (-jnp.arange(ROT_QUARTER, dtype=f32) / ROT_QUARTER)
    ang_r = (pos // GRID_W).astype(f32)[:, None] * inv[None, :]
    ang_c = (pos % GRID_W).astype(f32)[:, None] * inv[None, :]
    cos = jnp.concatenate([jnp.cos(ang_r)] * 2 + [jnp.cos(ang_c)] * 2, axis=-1)
    sin = jnp.concatenate([-jnp.sin(ang_r), jnp.sin(ang_r), -jnp.sin(ang_c), jnp.sin(ang_c)], axis=-1)
    return jnp.tile(cos, (1, 2)), jnp.tile(sin, (1, 2))


POOL_HALO = 8


def _pool_kernel(prev_ref, x_ref, next_ref, w_ref, scale_ref, o_ref, ext_ref, *, tm, L):
    i = pl.program_id(0)
    n = pl.num_programs(0)
    ext_ref[0:POOL_HALO, :] = jnp.where(i > 0, prev_ref[...], 0.0)
    ext_ref[POOL_HALO:POOL_HALO + tm, :] = x_ref[...]
    ext_ref[POOL_HALO + tm:, :] = jnp.where(i < n - 1, next_ref[...], 0.0)

    def sh(d):
        return ext_ref[POOL_HALO + d:POOL_HALO + d + tm, :]

    x = x_ref[...]
    sums = []
    acc = None
    prev_w = 0
    for w in POOL_WINDOWS:
        for d in list(range(-(w // 2), -(prev_w // 2))) + list(range(prev_w // 2, w // 2)):
            term = x if d == 0 else sh(d)
            acc = term if acc is None else acc + term
        sums.append(acc)
        prev_w = w
    lane = lax.broadcasted_iota(jnp.int32, (1, POOL_DIM), 1)
    t = i * tm + lax.broadcasted_iota(jnp.int32, (tm, 1), 0)
    gdim = POOL_DIM // POOL_GROUPS
    total = sums[-1]
    count = None
    for gi in range(POOL_GROUPS - 1, -1, -1):
        w = POOL_WINDOWS[gi]
        c = (jnp.minimum(t - w // 2 + w, L) - jnp.maximum(t - w // 2, 0)).astype(f32)
        if count is None:
            count = jnp.broadcast_to(c, (tm, POOL_DIM))
        else:
            sel = lane < gdim * (gi + 1)
            total = jnp.where(sel, sums[gi], total)
            count = jnp.where(sel, c, count)
    m = (total / count - x).astype(bf16)
    y = jnp.dot(m, w_ref[...], preferred_element_type=f32) * scale_ref[...]
    o_ref[...] = y.astype(bf16)


def pool_mixer(p, w_blockdiag, scale):
    L = p.shape[0]
    tm = _row_tile(L)
    hb = tm // POOL_HALO
    return pl.pallas_call(
        functools.partial(_pool_kernel, tm=tm, L=L),
        grid=(L // tm,),
        in_specs=[
            pl.BlockSpec((POOL_HALO, POOL_DIM), lambda i: (jnp.maximum(i * hb - 1, 0), 0)),
            pl.BlockSpec((tm, POOL_DIM), lambda i: (i, 0)),
            pl.BlockSpec((POOL_HALO, POOL_DIM), lambda i: (jnp.minimum((i + 1) * hb, L // POOL_HALO - 1), 0)),
            pl.BlockSpec((POOL_DIM, POOL_DIM), lambda i: (0, 0)),
            pl.BlockSpec((1, POOL_DIM), lambda i: (0, 0)),
        ],
        out_specs=pl.BlockSpec((tm, POOL_DIM), lambda i: (i, 0)),
        out_shape=jax.ShapeDtypeStruct((L, POOL_DIM), bf16),
        scratch_shapes=[pltpu.VMEM((tm + 2 * POOL_HALO, POOL_DIM), f32)],
        compiler_params=_params("parallel"),
        name="pool_mixer",
    )(p, p, p, w_blockdiag, scale)


def pool_blockdiag(pool_w):
    gdim = POOL_DIM // POOL_GROUPS
    w = jnp.zeros((POOL_DIM, POOL_DIM), f32)
    for g in range(POOL_GROUPS):
        w = w.at[g * gdim:(g + 1) * gdim, g * gdim:(g + 1) * gdim].set(pool_w[g])
    return w.astype(bf16)


NA_RB = 4
NA_QT = NA_RB * GRID_W
NA_WIN = NA_RB + NA_KH
NA_KB = NA_WIN // NA_RB
NA_DR_MASKED = 2 * NA_KH - 1
NA_VARIANTS = 3


def na_bias_tiles(rpb):
    period = 2 * GRID_W
    w = jnp.zeros(rpb.shape[:-1] + (period,), f32)
    w = w.at[..., :NA_KW].set(rpb[..., NA_KW - 1:])
    w = w.at[..., period - (NA_KW - 1):].set(rpb[..., :NA_KW - 1])
    t = jnp.tile(w, GRID_W)[..., :GRID_W * (period - 1)]
    t = t.reshape(rpb.shape[:-1] + (GRID_W, period - 1))[..., :GRID_W] * LOG2E
    qcol = np.arange(GRID_W)[:, None]
    kcol = np.arange(GRID_W)[None, :]
    ws = np.clip(qcol - NA_KW // 2, 0, GRID_W - NA_KW)
    valid = (kcol >= ws) & (kcol < ws + NA_KW)
    t = jnp.where(jnp.asarray(valid), t, NEG_INF)
    masked = jnp.full(rpb.shape[:2] + (1, GRID_W, GRID_W), NEG_INF, f32)
    return jnp.concatenate([t, masked], axis=2)


def _na_key_base(g, nb):
    return jnp.clip(g - 1, 0, nb - NA_KB)


def _na_dr_maps(rows):
    nb = rows // NA_RB
    maps = []
    for g in range(nb):
        kb = NA_RB * min(max(g - 1, 0), nb - NA_KB)
        m = np.full((NA_RB, NA_WIN), NA_DR_MASKED)
        for j in range(NA_RB):
            r = NA_RB * g + j
            rs = min(max(r - NA_KH // 2, 0), rows - NA_KH)
            for i in range(rs - kb, rs - kb + NA_KH):
                m[j, i] = kb + i - r + NA_KH - 1
        maps.append(m)
    variants = [maps[0], maps[1], maps[nb - 1]]
    for g in range(nb):
        assert (maps[g] == variants[0 if g == 0 else 2 if g == nb - 1 else 1]).all()
    return variants


def _with_ones(v, lo_half, sub):
    ones = jnp.ones_like(v)
    return jnp.where(lo_half, v, ones) if sub == 0 else jnp.where(lo_half, ones, v)


def _softmax_pv(s_parts, v_parts):
    m = functools.reduce(jnp.maximum, [jnp.max(s, axis=-1, keepdims=True) for s in s_parts])
    out = None
    for s, v in zip(s_parts, v_parts):
        pv = jnp.dot(jnp.exp2(s - m).astype(bf16), v, preferred_element_type=f32)
        out = pv if out is None else out + pv
    return out / pltpu.roll(out, HEAD_DIM, 1)


def _na_kernel(q_ref, k0_ref, k1_ref, k2_ref, v0_ref, v1_ref, v2_ref, kc_ref, vc_ref, tiles_ref, o_ref,
               bias_scr, *, nb, dr_maps):
    g = pl.program_id(0)

    @pl.when(g == 0)
    def _():
        for var in range(NA_VARIANTS):
            for j in range(NA_RB):
                for i in range(NA_WIN):
                    bias_scr[var, :, GRID_W * j:GRID_W * (j + 1), GRID_W * i:GRID_W * (i + 1)] = (
                        tiles_ref[:, int(dr_maps[var][j, i])])

    var = jnp.where(g == 0, 0, jnp.where(g == nb - 1, 2, 1))
    lo_half = _lo_half()
    k_refs = (k0_ref, k1_ref, k2_ref)
    v_refs = (v0_ref, v1_ref, v2_ref)
    for hp in range(N_PAIRS):
        sl = slice(LANES * hp, LANES * (hp + 1))
        q2 = q_ref[:, sl]
        values = [vr[:, sl] for vr in v_refs] + [vc_ref[:, sl]]
        outs = []
        for sub in range(2):
            values_sub = [_with_ones(v, lo_half, sub) for v in values]
            qm = jnp.where(lo_half if sub == 0 else jnp.logical_not(lo_half), q2, jnp.zeros_like(q2))
            scores = []
            for kb in range(NA_KB):
                s = lax.dot_general(qm, k_refs[kb][:, sl], NT_DIMS, preferred_element_type=f32)
                tb = bias_scr[var, 2 * hp + sub, :, NA_QT * kb:NA_QT * (kb + 1)]
                scores.append(jnp.where(tb < 0.5 * NEG_INF, NEG_INF, s + tb))
            scores.append(lax.dot_general(qm, kc_ref[:, sl], NT_DIMS, preferred_element_type=f32))
            outs.append(_softmax_pv(scores, values_sub))
        o_ref[:, sl] = jnp.where(lo_half, outs[0], outs[1]).astype(bf16)


def neighborhood_attention(na, na_ctx, tiles):
    L = na.shape[0]
    rows = L // GRID_W
    nb = rows // NA_RB
    assert rows % NA_RB == 0 and nb > NA_KB
    ctx = na_ctx.shape[0]
    blk = lambda off, col: pl.BlockSpec((NA_QT, NA_DIM), lambda g: (_na_key_base(g, nb) + off, col))
    return pl.pallas_call(
        functools.partial(_na_kernel, nb=nb, dr_maps=_na_dr_maps(rows)),
        grid=(nb,),
        in_specs=[
            pl.BlockSpec((NA_QT, NA_DIM), lambda g: (g, 0)),
            blk(0, 1), blk(1, 1), blk(2, 1),
            blk(0, 2), blk(1, 2), blk(2, 2),
            pl.BlockSpec((ctx, NA_DIM), lambda g: (0, 1)),
            pl.BlockSpec((ctx, NA_DIM), lambda g: (0, 2)),
            pl.BlockSpec(tiles.shape, lambda g: (0, 0, 0, 0)),
        ],
        out_specs=pl.BlockSpec((NA_QT, NA_DIM), lambda g: (g, 0)),
        out_shape=jax.ShapeDtypeStruct((L, NA_DIM), bf16),
        scratch_shapes=[pltpu.VMEM((NA_VARIANTS, NA_HEADS, NA_QT, NA_WIN * GRID_W), f32)],
        compiler_params=_params("arbitrary"),
        name="neighborhood_attention",
    )(na, na, na, na, na, na, na, na_ctx, na_ctx, tiles)


def _ctx_attn_kernel(q_ref, k_ref, v_ref, o_ref):
    lo_half = _lo_half()
    for hp in range(N_PAIRS):
        sl = slice(LANES * hp, LANES * (hp + 1))
        q2 = q_ref[:, sl]
        k2 = k_ref[:, sl]
        v2 = v_ref[:, sl]
        outs = []
        for sub in range(2):
            qm = jnp.where(lo_half if sub == 0 else jnp.logical_not(lo_half), q2, jnp.zeros_like(q2))
            s = lax.dot_general(qm, k2, NT_DIMS, preferred_element_type=f32)
            outs.append(_softmax_pv([s], [_with_ones(v2, lo_half, sub)]))
        o_ref[:, sl] = jnp.where(lo_half, outs[0], outs[1]).astype(bf16)


def context_attention(na_ctx):
    ctx = na_ctx.shape[0]
    col = lambda c: pl.BlockSpec((ctx, NA_DIM), lambda i: (0, c))
    return pl.pallas_call(
        _ctx_attn_kernel,
        grid=(1,),
        in_specs=[col(0), col(1), col(2)],
        out_specs=pl.BlockSpec((ctx, NA_DIM), lambda i: (0, 0)),
        out_shape=jax.ShapeDtypeStruct((ctx, NA_DIM), bf16),
        compiler_params=_params("arbitrary"),
        name="context_attention",
    )(na_ctx, na_ctx, na_ctx)


RET_STEP = 4


def _pair_blockmask():
    r = lax.broadcasted_iota(jnp.int32, (LANES, LANES), 0) < HEAD_DIM
    c = lax.broadcasted_iota(jnp.int32, (LANES, LANES), 1) < HEAD_DIM
    return r == c


def _ret_state_kernel(kf_ref, vf_ref, kb_ref, vb_ref, lgf_ref, lgb_ref, s0f_ref, s0b_ref,
                      sf_out, sb_out, sf_fin, sb_fin, sf_scr, sb_scr, *, cs):
    n = pl.program_id(0)

    @pl.when(n == 0)
    def _():
        sf_scr[...] = s0f_ref[...]
        sb_scr[...] = s0b_ref[...]

    c = lax.broadcasted_iota(jnp.int32, (RET_CHUNK, 1), 0).astype(f32)
    blockmask = _pair_blockmask()
    dirs = (
        (kf_ref, vf_ref, lgf_ref, sf_scr, sf_out, sf_fin, RET_CHUNK - 1.0 - c, range(cs)),
        (kb_ref, vb_ref, lgb_ref, sb_scr, sb_out, sb_fin, c, range(cs - 1, -1, -1)),
    )
    for k_ref, v_ref, lg_ref, scr, out, fin, steps, order in dirs:
        lg = lg_ref[...]
        kdec = jnp.exp(steps * lg)
        cdec = jnp.exp(float(RET_CHUNK) * lg)
        state = [scr[hp] for hp in range(N_PAIRS)]
        for ci in order:
            rows = slice(RET_CHUNK * ci, RET_CHUNK * (ci + 1))
            kd = (k_ref[rows, :].astype(f32) * kdec).astype(bf16)
            v = v_ref[rows, :]
            for hp in range(N_PAIRS):
                sl = slice(LANES * hp, LANES * (hp + 1))
                out[ci, hp] = state[hp].astype(bf16)
                kv = lax.dot_general(kd[:, sl], v[:, sl], TN_DIMS, preferred_element_type=f32)
                state[hp] = state[hp] * cdec[:, sl] + jnp.where(blockmask, kv, 0.0)
        for hp in range(N_PAIRS):
            scr[hp] = state[hp]
            fin[hp] = state[hp]


def retention_states(ret, lgf_lane, lgb_lane, s0f, s0b):
    L = ret.shape[0]
    N = L // RET_CHUNK
    cs = min(RET_STEP, N)
    ns = N // cs
    blk = lambda f, col: pl.BlockSpec((cs * RET_CHUNK, RET_DIM), lambda n: (f(n), col))
    fwd = lambda n: n
    bwd = lambda n: ns - 1 - n
    vec = pl.BlockSpec((1, RET_DIM), lambda n: (0, 0))
    st = pl.BlockSpec((N_PAIRS, LANES, LANES), lambda n: (0, 0, 0))
    st_shape = jax.ShapeDtypeStruct((N_PAIRS, LANES, LANES), f32)
    all_shape = jax.ShapeDtypeStruct((N, N_PAIRS, LANES, LANES), bf16)
    return pl.pallas_call(
        functools.partial(_ret_state_kernel, cs=cs),
        grid=(ns,),
        in_specs=[blk(fwd, 1), blk(fwd, 2), blk(bwd, 1), blk(bwd, 2), vec, vec, st, st],
        out_specs=[
            pl.BlockSpec((cs, N_PAIRS, LANES, LANES), lambda n: (n, 0, 0, 0)),
            pl.BlockSpec((cs, N_PAIRS, LANES, LANES), lambda n: (ns - 1 - n, 0, 0, 0)),
            st, st,
        ],
        out_shape=[all_shape, all_shape, st_shape, st_shape],
        scratch_shapes=[pltpu.VMEM((N_PAIRS, LANES, LANES), f32), pltpu.VMEM((N_PAIRS, LANES, LANES), f32)],
        compiler_params=_params("arbitrary"),
        name="retention_states",
    )(ret, ret, ret, ret, lgf_lane, lgb_lane, s0f, s0b)


def _ret_out_kernel(lgf_s, lgb_s, q_ref, k_ref, v_ref, gate_ref, sf_ref, sb_ref, lgf_ref, lgb_ref, gn_ref, o_ref,
                    *, cs):
    lo_half = _lo_half()
    diff = (lax.broadcasted_iota(jnp.int32, (RET_CHUNK, RET_CHUNK), 0)
            - lax.broadcasted_iota(jnp.int32, (RET_CHUNK, RET_CHUNK), 1)).astype(f32)
    c = lax.broadcasted_iota(jnp.int32, (RET_CHUNK, 1), 0).astype(f32)
    qdec_f = jnp.exp((c + 1.0) * lgf_ref[...])
    qdec_b = jnp.exp((RET_CHUNK - c) * lgb_ref[...])
    decay = [jnp.where(diff >= 0, jnp.exp(diff * lgf_s[h]), 0.0) + jnp.where(diff <= 0, jnp.exp(-diff * lgb_s[h]), 0.0)
             for h in range(RET_HEADS)]
    inv_hd = 1.0 / HEAD_DIM

    def head_mean(t):
        m0 = jnp.sum(jnp.where(lo_half, t, 0.0), axis=-1, keepdims=True) * inv_hd
        m1 = jnp.sum(jnp.where(lo_half, 0.0, t), axis=-1, keepdims=True) * inv_hd
        return jnp.where(lo_half, m0, m1)

    for ci in range(cs):
        rows = slice(RET_CHUNK * ci, RET_CHUNK * (ci + 1))
        q = q_ref[rows, :]
        qf32 = q.astype(f32)
        q_f = (qf32 * qdec_f).astype(bf16)
        q_b = (qf32 * qdec_b).astype(bf16)
        for hp in range(N_PAIRS):
            sl = slice(LANES * hp, LANES * (hp + 1))
            q2 = q[:, sl]
            k2 = k_ref[rows, sl]
            v2 = v_ref[rows, sl]
            lhs = []
            rhs = []
            for sub in range(2):
                hmask = lo_half if sub == 0 else jnp.logical_not(lo_half)
                qm = jnp.where(hmask, q2, jnp.zeros_like(q2))
                a = lax.dot_general(qm, k2, NT_DIMS, preferred_element_type=f32)
                lhs.append((a * decay[2 * hp + sub]).astype(bf16))
                rhs.append(jnp.where(hmask, v2, jnp.zeros_like(v2)))
            lhs += [q_f[:, sl], q_b[:, sl]]
            rhs += [sf_ref[ci, hp], sb_ref[ci, hp]]
            y = jnp.dot(jnp.concatenate(lhs, axis=1), jnp.concatenate(rhs, axis=0), preferred_element_type=f32)
            d = y - head_mean(y)
            yn = d * lax.rsqrt(head_mean(d * d) + EPS) * gn_ref[:, sl]
            o_ref[rows, sl] = (_silu(gate_ref[rows, sl]) * yn).astype(bf16)


def retention_outputs(ret, gate, sf, sb, lgf, lgb, lgf_lane, lgb_lane, gn):
    L = ret.shape[0]
    N = L // RET_CHUNK
    cs = min(RET_STEP, N)
    col = lambda c: pl.BlockSpec((cs * RET_CHUNK, RET_DIM), lambda n, *_: (n, c))
    vec = pl.BlockSpec((1, RET_DIM), lambda n, *_: (0, 0))
    st = pl.BlockSpec((cs, N_PAIRS, LANES, LANES), lambda n, *_: (n, 0, 0, 0))
    return pl.pallas_call(
        functools.partial(_ret_out_kernel, cs=cs),
        grid_spec=pltpu.PrefetchScalarGridSpec(
            num_scalar_prefetch=2,
            grid=(N // cs,),
            in_specs=[col(0), col(1), col(2), col(0), st, st, vec, vec, vec],
            out_specs=pl.BlockSpec((cs * RET_CHUNK, RET_DIM), lambda n, *_: (n, 0)),
        ),
        out_shape=jax.ShapeDtypeStruct((L, RET_DIM), bf16),
        compiler_params=_params("parallel"),
        name="retention_outputs",
    )(lgf, lgb, ret, ret, ret, gate, sf, sb, lgf_lane, lgb_lane, gn)


FFN_HALO = 16
FFN_TF = 256


def _out_ffn_kernel(hp_ref, h_ref, hn_ref, pp_ref, p_ref, pn_ref, np_ref, n_ref, nn_ref, rp_ref, r_ref, rn_ref,
                    mod_ref, g_ref, wo_ref, wu_ref, cw_ref, cb_ref, wd_ref, fg_ref, o_ref, aext, gbuf, *, tm, final):
    i = pl.program_id(0)
    ni = pl.num_programs(0)
    ext = tm + 2 * FFN_HALO

    def rows3(prev, cur, nxt):
        return jnp.concatenate([prev[...], cur[...], nxt[...]], axis=0)

    mixed = jnp.concatenate([rows3(pp_ref, p_ref, pn_ref), rows3(np_ref, n_ref, nn_ref),
                             rows3(rp_ref, r_ref, rn_ref)], axis=1)
    hx = rows3(hp_ref, h_ref, hn_ref) + mod_ref[2:3, :] * jnp.dot(mixed, wo_ref[...], preferred_element_type=f32)
    o_ref[...] = hx[FFN_HALO:FFN_HALO + tm]
    a = hx * lax.rsqrt(jnp.mean(hx * hx, axis=-1, keepdims=True) + EPS) * g_ref[...]
    a = a * (1.0 + mod_ref[4:5, :]) + mod_ref[3:4, :]
    row = lax.broadcasted_iota(jnp.int32, (ext, 1), 0)
    inside = jnp.logical_and(jnp.logical_or(row >= FFN_HALO, i > 0), jnp.logical_or(row < FFN_HALO + tm, i < ni - 1))
    aext[...] = jnp.where(inside, a, 0.0).astype(bf16)
    ae = aext[...]

    def branch(lo):
        u = jnp.dot(ae, wu_ref[:, lo:lo + FFN_TF], preferred_element_type=f32)
        up = pltpu.roll(u, 1, 0)[FFN_HALO:FFN_HALO + tm]
        un = pltpu.roll(u, ext - 1, 0)[FFN_HALO:FFN_HALO + tm]
        uc = u[FFN_HALO:FFN_HALO + tm]
        cw = cw_ref[:, lo:lo + FFN_TF]
        return up * cw[0:1, :] + uc * cw[1:2, :] + un * cw[2:3, :] + cb_ref[:, lo:lo + FFN_TF]

    for s in range(D_FF // FFN_TF):
        lo = FFN_TF * s
        gbuf[:, lo:lo + FFN_TF] = (_silu(branch(lo)) * branch(D_FF + lo)).astype(bf16)
    y = jnp.dot(gbuf[...], wd_ref[...], preferred_element_type=f32)
    out = o_ref[...] + mod_ref[5:6, :] * y
    if final:
        out = out * lax.rsqrt(jnp.mean(out * out, axis=-1, keepdims=True) + EPS) * fg_ref[...]
    o_ref[...] = out


def out_proj_ffn(h, pool_x, na_x, ret_x, mod, g2, w_out, w_up, conv_w, conv_b, w_down, layer, final_g, final):
    L = h.shape[0]
    tm = _row_tile(L)
    hb = tm // FFN_HALO

    def halo3(n):
        return [
            pl.BlockSpec((FFN_HALO, n), lambda i: (jnp.maximum(i * hb - 1, 0), 0)),
            pl.BlockSpec((tm, n), lambda i: (i, 0)),
            pl.BlockSpec((FFN_HALO, n), lambda i: (jnp.minimum((i + 1) * hb, L // FFN_HALO - 1), 0)),
        ]

    once = dict(pipeline_mode=pl.Buffered(1))
    const = lambda arr: pl.BlockSpec(arr.shape, lambda i: (0,) * arr.ndim, **once)
    stack = lambda arr: _layer_spec(arr, layer, **once)
    consts = (mod, g2, w_out, w_up, conv_w, conv_b, w_down, final_g)
    const_specs = [const(mod), const(g2), stack(w_out), stack(w_up), const(conv_w), const(conv_b), stack(w_down),
                   const(final_g)]
    return pl.pallas_call(
        functools.partial(_out_ffn_kernel, tm=tm, final=final),
        grid=(L // tm,),
        in_specs=halo3(D_MODEL) + halo3(POOL_DIM) + halo3(NA_DIM) + halo3(RET_DIM) + const_specs,
        out_specs=pl.BlockSpec((tm, D_MODEL), lambda i: (i, 0)),
        out_shape=jax.ShapeDtypeStruct((L, D_MODEL), f32),
        scratch_shapes=[pltpu.VMEM((tm + 2 * FFN_HALO, D_MODEL), bf16), pltpu.VMEM((tm, D_FF), bf16)],
        compiler_params=_params("parallel"),
        name="out_proj_ffn",
    )(h, h, h, pool_x, pool_x, pool_x, na_x, na_x, na_x, ret_x, ret_x, ret_x, *consts)


def kernel(x, c, ctx, c_ctx, w_mod, b_mod, norm1_g, w_in, pool_w, pool_scale, na_rpb, ret_decay_fwd, ret_decay_bwd,
           ret_gn_g, w_out, norm2_g, w_up, conv_w, conv_b, w_down, final_g):
    B, L, _ = x.shape
    assert B == 1
    n_ctx = ctx.shape[1]
    h = x[0]
    hc = ctx[0]
    mods = mod_vectors(c, c_ctx, w_mod, b_mod)
    cos, sin = rope_tables(L)
    cos_id = jnp.ones((n_ctx, LANES), f32)
    sin_id = jnp.zeros((n_ctx, LANES), f32)
    zero_state = jnp.zeros((N_PAIRS, LANES, LANES), f32)
    fg = final_g.reshape(1, D_MODEL)
    wi = w_in.astype(bf16)
    wo = w_out.astype(bf16)
    wu = w_up.astype(bf16)
    wd = w_down.astype(bf16)
    bias_tiles = na_bias_tiles(na_rpb)
    for l in range(DEPTH):
        need_ctx = l < DEPTH - 1
        mx = mods[l, 0].reshape(6, D_MODEL)
        mc = mods[l, 1].reshape(6, D_MODEL)
        g1 = norm1_g[l].reshape(1, D_MODEL)
        g2 = norm2_g[l].reshape(1, D_MODEL)
        cb = conv_b[l].reshape(1, 2 * D_FF)
        wp = pool_blockdiag(pool_w[l])
        ps = pool_scale[l].reshape(1, POOL_DIM)
        lgf = ret_decay_fwd[l].astype(f32)
        lgb = ret_decay_bwd[l].astype(f32)
        lgf_lane = jnp.repeat(lgf, HEAD_DIM).reshape(1, RET_DIM)
        lgb_lane = jnp.repeat(lgb, HEAD_DIM).reshape(1, RET_DIM)
        gn = ret_gn_g[l].reshape(1, RET_DIM)
        ffn_w = (g2, wo, wu, conv_w[l], cb, wd, l, fg)

        pc, nac, retc, gatec = in_proj(hc, g1, mc, wi, l, cos_id, sin_id)
        px, nax, retx, gatex = in_proj(h, g1, mx, wi, l, cos, sin)

        sfc, sbc, sf0, sb0 = retention_states(retc, lgf_lane, lgb_lane, zero_state, zero_state)
        sfx, sbx, _, _ = retention_states(retx, lgf_lane, lgb_lane, sf0, sb0)

        pool_x = pool_mixer(px, wp, ps)
        na_x = neighborhood_attention(nax, nac, bias_tiles[l])
        ret_x = retention_outputs(retx, gatex, sfx, sbx, lgf, lgb, lgf_lane, lgb_lane, gn)
        h = out_proj_ffn(h, pool_x, na_x, ret_x, mx, *ffn_w, final=not need_ctx)

        if need_ctx:
            pool_c = pool_mixer(pc, wp, ps)
            na_c = context_attention(nac)
            ret_c = retention_outputs(retc, gatec, sfc, sbc, lgf, lgb, lgf_lane, lgb_lane, gn)
            hc = out_proj_ffn(hc, pool_c, na_c, ret_c, mc, *ffn_w, final=False)
    return h[None]
```

```python
import functools

import numpy as np
import jax
import jax.numpy as jnp
from jax import lax
from jax.experimental import pallas as pl
from jax.experimental.pallas import tpu as pltpu

f32 = jnp.float32
bf16 = jnp.bfloat16

D_MODEL = 1024
DEPTH = 4
GRID_W = 64
HEAD_DIM = 64
POOL_GROUPS = 4
POOL_WINDOWS = (2, 4, 8, 16)
POOL_DIM = 256
NA_HEADS = 6
NA_DIM = 384
NA_KH = 8
NA_KW = 16
RET_HEADS = 6
RET_DIM = 384
RET_CHUNK = 128
D_FF = 2816
ROPE_BASE = 10000.0
ROT_QUARTER = HEAD_DIM // 4
EPS = 1e-6
NEG_INF = -1e30
IN_DIM = POOL_DIM + 3 * NA_DIM + 4 * RET_DIM
OFF_NA = POOL_DIM
OFF_RET = OFF_NA + 3 * NA_DIM
OFF_GATE = OFF_RET + 3 * RET_DIM

LANES = 128
N_PAIRS = NA_HEADS // 2
VMEM_LIMIT = 56 * 1024 * 1024

LOG2E = float(np.log2(np.e))
NT_DIMS = (((1,), (1,)), ((), ()))
TN_DIMS = (((0,), (0,)), ((), ()))


def _params(*sem):
    return pltpu.CompilerParams(dimension_semantics=sem, vmem_limit_bytes=VMEM_LIMIT)


def _layer_spec(stack, layer, **kw):
    return pl.BlockSpec((None,) + stack.shape[1:], lambda i: (layer, 0, 0), **kw)


def _row_tile(L):
    return min(L, 512)


def _silu(x):
    return x * jax.nn.sigmoid(x)


def _lo_half():
    return lax.broadcasted_iota(jnp.int32, (1, LANES), 1) < HEAD_DIM


def _mod_kernel(s_ref, w_ref, b_ref, o_ref):
    a = _silu(s_ref[...]).astype(bf16)
    o_ref[...] = jnp.dot(a, w_ref[...].astype(bf16), preferred_element_type=f32) + b_ref[...]


def mod_vectors(c, c_ctx, w_mod, b_mod):
    s = jnp.zeros((8, D_MODEL), f32).at[0].set(c[0]).at[1].set(c_ctx)
    return pl.pallas_call(
        _mod_kernel,
        grid=(DEPTH, 6),
        in_specs=[
            pl.BlockSpec((8, D_MODEL), lambda l, j: (0, 0)),
            pl.BlockSpec((None, D_MODEL, D_MODEL), lambda l, j: (l, 0, j)),
            pl.BlockSpec((None, 1, D_MODEL), lambda l, j: (l, 0, j)),
        ],
        out_specs=pl.BlockSpec((None, 8, D_MODEL), lambda l, j: (l, 0, j)),
        out_shape=jax.ShapeDtypeStruct((DEPTH, 8, 6 * D_MODEL), f32),
        compiler_params=_params("parallel", "parallel"),
        name="mod_vectors",
    )(s, w_mod, b_mod.reshape(DEPTH, 1, 6 * D_MODEL))


def _in_proj_kernel(h_ref, g_ref, mod_ref, w_ref, cos_row_ref, sin_row_ref, cos_col_ref, sin_col_ref,
                    p_ref, na_ref, ret_ref, gate_ref):
    h = h_ref[...]
    y = h * lax.rsqrt(jnp.mean(h * h, axis=-1, keepdims=True) + EPS) * g_ref[...]
    a = (y * (1.0 + mod_ref[1:2, :]) + mod_ref[0:1, :]).astype(bf16)

    def proj(lo, n):
        return jnp.dot(a, w_ref[:, lo:lo + n], preferred_element_type=f32)

    p_ref[...] = proj(0, POOL_DIM)
    na = proj(OFF_NA, 3 * NA_DIM)
    na_ref[:, 0:NA_DIM] = (na[:, 0:NA_DIM] * (HEAD_DIM ** -0.5 * LOG2E)).astype(bf16)
    na_ref[:, NA_DIM:] = na[:, NA_DIM:].astype(bf16)
    lane = lax.broadcasted_iota(jnp.int32, (1, LANES), 1)
    first = lane % (2 * ROT_QUARTER) < ROT_QUARTER
    row_lane = lane % HEAD_DIM < HEAD_DIM // 2

    def table(row_ref, col_ref):
        col = col_ref[...]
        return jnp.concatenate(
            [jnp.where(row_lane, jnp.broadcast_to(row_ref[j:j + 1, :], col.shape), col)
             for j in range(row_ref.shape[0])], axis=0)

    cos = table(cos_row_ref, cos_col_ref)
    sin = table(sin_row_ref, sin_col_ref)
    qk = proj(OFF_RET, 2 * RET_DIM)
    for slab in range(2 * RET_DIM // LANES):
        lo = LANES * slab
        x = qk[:, lo:lo + LANES]
        partner = jnp.where(first, pltpu.roll(x, LANES - ROT_QUARTER, 1), pltpu.roll(x, ROT_QUARTER, 1))
        x = x * cos + partner * sin
        if lo >= RET_DIM:
            x = x * HEAD_DIM ** -0.5
        ret_ref[:, lo:lo + LANES] = x.astype(bf16)
    vg = proj(OFF_RET + 2 * RET_DIM, 2 * RET_DIM)
    ret_ref[:, 2 * RET_DIM:] = vg[:, :RET_DIM].astype(bf16)
    gate_ref[...] = vg[:, RET_DIM:]


def in_proj(h, g, mod, w_in, layer, rope):
    L = h.shape[0]
    tm = _row_tile(L)
    assert tm % GRID_W == 0
    row = lambda n: pl.BlockSpec((tm, n), lambda i: (i, 0))
    full = lambda a: pl.BlockSpec(a.shape, lambda i: (0,) * a.ndim)
    grid_rows = pl.BlockSpec((tm // GRID_W, LANES), lambda i: (i, 0))
    grid_cols = pl.BlockSpec((GRID_W, LANES), lambda i: (0, 0))
    return pl.pallas_call(
        _in_proj_kernel,
        grid=(L // tm,),
        in_specs=[row(D_MODEL), full(g), full(mod), _layer_spec(w_in, layer),
                  grid_rows, grid_rows, grid_cols, grid_cols],
        out_specs=[row(POOL_DIM), row(3 * NA_DIM), row(3 * RET_DIM), row(RET_DIM)],
        out_shape=[
            jax.ShapeDtypeStruct((L, POOL_DIM), f32),
            jax.ShapeDtypeStruct((L, 3 * NA_DIM), bf16),
            jax.ShapeDtypeStruct((L, 3 * RET_DIM), bf16),
            jax.ShapeDtypeStruct((L, RET_DIM), f32),
        ],
        compiler_params=_params("parallel"),
        name="in_proj",
    )(h, g, mod, w_in, *rope)


def rope_tables(rows):
    inv = ROPE_BASE ** (-jnp.arange(ROT_QUARTER, dtype=f32) / ROT_QUARTER)

    def factors(n):
        ang = jnp.arange(n, dtype=f32)[:, None] * inv[None, :]
        cos = jnp.tile(jnp.cos(ang), (1, LANES // ROT_QUARTER))
        sin = jnp.tile(jnp.concatenate([-jnp.sin(ang), jnp.sin(ang)], axis=-1), (1, LANES // (2 * ROT_QUARTER)))
        return cos, sin

    return factors(rows) + factors(GRID_W)


def rope_identity(rows):
    ones = lambda n: jnp.ones((n, LANES), f32)
    zeros = lambda n: jnp.zeros((n, LANES), f32)
    return ones(rows), zeros(rows), ones(GRID_W), zeros(GRID_W)


POOL_HALO = 8


def _pool_kernel(prev_ref, x_ref, next_ref, w_ref, scale_ref, o_ref, lvl_ref, *, tm, L):
    i = pl.program_id(0)
    n = pl.num_programs(0)
    ext = tm + 2 * POOL_HALO
    margin = POOL_HALO
    assert POOL_WINDOWS == (2, 4, 8, 16)
    zeros = jnp.zeros((margin, POOL_DIM), f32)
    for k in range(3):
        lvl_ref[k, 0:margin, :] = zeros
        lvl_ref[k, margin + ext:, :] = zeros
    lvl_ref[0, margin:margin + POOL_HALO, :] = jnp.where(i > 0, prev_ref[...], 0.0)
    lvl_ref[0, margin + POOL_HALO:margin + POOL_HALO + tm, :] = x_ref[...]
    lvl_ref[0, margin + POOL_HALO + tm:margin + ext, :] = jnp.where(i < n - 1, next_ref[...], 0.0)

    def shifted(k, d, rows=ext, start=margin):
        return lvl_ref[k, start + d:start + d + rows, :]

    lvl_ref[1, margin:margin + ext, :] = shifted(0, -1) + shifted(0, 0)
    lvl_ref[2, margin:margin + ext, :] = shifted(1, -1) + shifted(1, 1)
    lvl_ref[3, margin:margin + ext, :] = shifted(2, -2) + shifted(2, 2)
    centre = margin + POOL_HALO
    x = x_ref[...]
    sums = [shifted(k, 0, tm, centre) for k in (1, 2, 3)]
    sums.append(shifted(3, -4, tm, centre) + shifted(3, 4, tm, centre))
    lane = lax.broadcasted_iota(jnp.int32, (1, POOL_DIM), 1)
    gdim = POOL_DIM // POOL_GROUPS
    total = sums[-1]
    half = jnp.full((1, POOL_DIM), POOL_WINDOWS[-1] // 2, jnp.int32)
    for gi in range(POOL_GROUPS - 2, -1, -1):
        sel = lane < gdim * (gi + 1)
        total = jnp.where(sel, sums[gi], total)
        half = jnp.where(sel, POOL_WINDOWS[gi] // 2, half)

    def finish(mean):
        m = (mean - x).astype(bf16)
        o_ref[...] = (jnp.dot(m, w_ref[...], preferred_element_type=f32) * scale_ref[...]).astype(bf16)

    clipped = jnp.logical_or(i == 0, i == n - 1)

    @pl.when(clipped)
    def _():
        t = i * tm + lax.broadcasted_iota(jnp.int32, (tm, 1), 0)
        count = jnp.minimum(t + half, L) - jnp.maximum(t - half, 0)
        finish(total / count.astype(f32))

    @pl.when(jnp.logical_not(clipped))
    def _():
        finish(total * (0.5 / half.astype(f32)))


def pool_mixer(p, w_blockdiag, scale):
    L = p.shape[0]
    tm = _row_tile(L)
    hb = tm // POOL_HALO
    return pl.pallas_call(
        functools.partial(_pool_kernel, tm=tm, L=L),
        grid=(L // tm,),
        in_specs=[
            pl.BlockSpec((POOL_HALO, POOL_DIM), lambda i: (jnp.maximum(i * hb - 1, 0), 0)),
            pl.BlockSpec((tm, POOL_DIM), lambda i: (i, 0)),
            pl.BlockSpec((POOL_HALO, POOL_DIM), lambda i: (jnp.minimum((i + 1) * hb, L // POOL_HALO - 1), 0)),
            pl.BlockSpec((POOL_DIM, POOL_DIM), lambda i: (0, 0)),
            pl.BlockSpec((1, POOL_DIM), lambda i: (0, 0)),
        ],
        out_specs=pl.BlockSpec((tm, POOL_DIM), lambda i: (i, 0)),
        out_shape=jax.ShapeDtypeStruct((L, POOL_DIM), bf16),
        scratch_shapes=[pltpu.VMEM((4, tm + 4 * POOL_HALO, POOL_DIM), f32)],
        compiler_params=_params("parallel"),
        name="pool_mixer",
    )(p, p, p, w_blockdiag, scale)


def pool_blockdiag(pool_w):
    gdim = POOL_DIM // POOL_GROUPS
    w = jnp.zeros((POOL_DIM, POOL_DIM), f32)
    for g in range(POOL_GROUPS):
        w = w.at[g * gdim:(g + 1) * gdim, g * gdim:(g + 1) * gdim].set(pool_w[g])
    return w.astype(bf16)


NA_RB = 4
NA_QT = NA_RB * GRID_W
NA_WIN = NA_RB + NA_KH
NA_KB = NA_WIN // NA_RB
NA_DR_MASKED = 2 * NA_KH - 1
NA_VARIANTS = 3


def na_bias_rows(rpb):
    assert LANES == 2 * GRID_W
    w = jnp.zeros(rpb.shape[:-1] + (LANES,), f32)
    w = w.at[..., :NA_KW].set(rpb[..., NA_KW - 1:])
    w = w.at[..., LANES - (NA_KW - 1):].set(rpb[..., :NA_KW - 1])
    return w * LOG2E


def _na_fill_bias(rows_ref, bias_scr, dr_maps):
    lane = lax.broadcasted_iota(jnp.int32, (GRID_W, LANES), 1)
    qcol = lax.broadcasted_iota(jnp.int32, (GRID_W, LANES), 0)
    kcol = lane % GRID_W
    ws = jnp.clip(qcol - NA_KW // 2, 0, GRID_W - NA_KW)
    in_window = jnp.logical_and(kcol >= ws, kcol < ws + NA_KW)
    masked = jnp.full((GRID_W, LANES), NEG_INF, f32)
    for h in range(NA_HEADS):
        for dr in range(NA_DR_MASKED + 1):
            if dr == NA_DR_MASKED:
                tile = masked
            else:
                row = jnp.broadcast_to(rows_ref[h, dr:dr + 1, :], (GRID_W, LANES))
                t = pltpu.roll(row, 0, 1, stride=1, stride_axis=0)
                t = jnp.where(lane < GRID_W, t, pltpu.roll(t, GRID_W, 1))
                tile = jnp.where(in_window, t, NEG_INF)
            for var in range(NA_VARIANTS):
                for j, i in zip(*np.nonzero(dr_maps[var] == dr)):
                    half = GRID_W * (int(i) % 2)
                    bias_scr[var, h, GRID_W * j:GRID_W * (j + 1), GRID_W * i:GRID_W * (i + 1)] = (
                        tile[:, half:half + GRID_W])


def _na_key_base(g, nb):
    return jnp.clip(g - 1, 0, nb - NA_KB)


def _na_dr_maps(rows):
    nb = rows // NA_RB
    maps = []
    for g in range(nb):
        kb = NA_RB * min(max(g - 1, 0), nb - NA_KB)
        m = np.full((NA_RB, NA_WIN), NA_DR_MASKED)
        for j in range(NA_RB):
            r = NA_RB * g + j
            rs = min(max(r - NA_KH // 2, 0), rows - NA_KH)
            for i in range(rs - kb, rs - kb + NA_KH):
                m[j, i] = kb + i - r + NA_KH - 1
        maps.append(m)
    variants = [maps[0], maps[1], maps[nb - 1]]
    for g in range(nb):
        assert (maps[g] == variants[0 if g == 0 else 2 if g == nb - 1 else 1]).all()
    return variants


def _with_ones(v, lo_half, sub):
    ones = jnp.ones_like(v)
    return jnp.where(lo_half, v, ones) if sub == 0 else jnp.where(lo_half, ones, v)


def _softmax_pv(s_parts, v_parts):
    maxes = [jnp.max(s, axis=-1, keepdims=True) for s in s_parts]
    m = functools.reduce(jnp.maximum, maxes)
    out = None
    for s, v, mk in zip(s_parts, v_parts, maxes):
        pv = jnp.dot(jnp.exp2(s - mk).astype(bf16), v, preferred_element_type=f32) * jnp.exp2(mk - m)
        out = pv if out is None else out + pv
    return out / pltpu.roll(out, HEAD_DIM, 1)


def _na_kernel(q_ref, k0_ref, k1_ref, k2_ref, v0_ref, v1_ref, v2_ref, kc_ref, vc_ref, rows_ref, o_ref,
               bias_scr, *, nb, dr_maps):
    g = pl.program_id(0)

    @pl.when(g == 0)
    def _():
        _na_fill_bias(rows_ref, bias_scr, dr_maps)

    var = jnp.where(g == 0, 0, jnp.where(g == nb - 1, 2, 1))
    lo_half = _lo_half()
    k_refs = (k0_ref, k1_ref, k2_ref)
    v_refs = (v0_ref, v1_ref, v2_ref)
    for hp in range(N_PAIRS):
        sl = slice(LANES * hp, LANES * (hp + 1))
        q2 = q_ref[:, sl]
        values = [vr[:, sl] for vr in v_refs] + [vc_ref[:, sl]]
        outs = []
        for sub in range(2):
            values_sub = [_with_ones(v, lo_half, sub) for v in values]
            qm = jnp.where(lo_half if sub == 0 else jnp.logical_not(lo_half), q2, jnp.zeros_like(q2))
            scores = []
            for kb in range(NA_KB):
                s = lax.dot_general(qm, k_refs[kb][:, sl], NT_DIMS, preferred_element_type=f32)
                tb = bias_scr[var, 2 * hp + sub, :, NA_QT * kb:NA_QT * (kb + 1)]
                scores.append(jnp.where(tb < 0.5 * NEG_INF, NEG_INF, s + tb))
            scores.append(lax.dot_general(qm, kc_ref[:, sl], NT_DIMS, preferred_element_type=f32))
            outs.append(_softmax_pv(scores, values_sub))
        o_ref[:, sl] = jnp.where(lo_half, outs[0], outs[1]).astype(bf16)


def neighborhood_attention(na, na_ctx, bias_rows):
    L = na.shape[0]
    rows = L // GRID_W
    nb = rows // NA_RB
    assert rows % NA_RB == 0 and nb > NA_KB
    ctx = na_ctx.shape[0]
    blk = lambda off, col: pl.BlockSpec((NA_QT, NA_DIM), lambda g: (_na_key_base(g, nb) + off, col))
    return pl.pallas_call(
        functools.partial(_na_kernel, nb=nb, dr_maps=_na_dr_maps(rows)),
        grid=(nb,),
        in_specs=[
            pl.BlockSpec((NA_QT, NA_DIM), lambda g: (g, 0)),
            blk(0, 1), blk(1, 1), blk(2, 1),
            blk(0, 2), blk(1, 2), blk(2, 2),
            pl.BlockSpec((ctx, NA_DIM), lambda g: (0, 1)),
            pl.BlockSpec((ctx, NA_DIM), lambda g: (0, 2)),
            pl.BlockSpec(bias_rows.shape, lambda g: (0, 0, 0)),
        ],
        out_specs=pl.BlockSpec((NA_QT, NA_DIM), lambda g: (g, 0)),
        out_shape=jax.ShapeDtypeStruct((L, NA_DIM), bf16),
        scratch_shapes=[pltpu.VMEM((NA_VARIANTS, NA_HEADS, NA_QT, NA_WIN * GRID_W), f32)],
        compiler_params=_params("arbitrary"),
        name="neighborhood_attention",
    )(na, na, na, na, na, na, na, na_ctx, na_ctx, bias_rows)


def _ctx_attn_kernel(q_ref, k_ref, v_ref, o_ref):
    lo_half = _lo_half()
    for hp in range(N_PAIRS):
        sl = slice(LANES * hp, LANES * (hp + 1))
        q2 = q_ref[:, sl]
        k2 = k_ref[:, sl]
        v2 = v_ref[:, sl]
        outs = []
        for sub in range(2):
            qm = jnp.where(lo_half if sub == 0 else jnp.logical_not(lo_half), q2, jnp.zeros_like(q2))
            s = lax.dot_general(qm, k2, NT_DIMS, preferred_element_type=f32)
            outs.append(_softmax_pv([s], [_with_ones(v2, lo_half, sub)]))
        o_ref[:, sl] = jnp.where(lo_half, outs[0], outs[1]).astype(bf16)


def context_attention(na_ctx):
    ctx = na_ctx.shape[0]
    col = lambda c: pl.BlockSpec((ctx, NA_DIM), lambda i: (0, c))
    return pl.pallas_call(
        _ctx_attn_kernel,
        grid=(1,),
        in_specs=[col(0), col(1), col(2)],
        out_specs=pl.BlockSpec((ctx, NA_DIM), lambda i: (0, 0)),
        out_shape=jax.ShapeDtypeStruct((ctx, NA_DIM), bf16),
        compiler_params=_params("arbitrary"),
        name="context_attention",
    )(na_ctx, na_ctx, na_ctx)


RET_STEP = 4


def _pair_blockmask():
    r = lax.broadcasted_iota(jnp.int32, (LANES, LANES), 0) < HEAD_DIM
    c = lax.broadcasted_iota(jnp.int32, (LANES, LANES), 1) < HEAD_DIM
    return r == c


def _ret_state_kernel(k_ref, v_ref, lgf_ref, lgb_ref, s0f_ref, s0b_ref, sf_out, sb_out, sf_fin, sb_fin, *, n_chunks):
    c = lax.broadcasted_iota(jnp.int32, (RET_CHUNK, 1), 0).astype(f32)
    blockmask = _pair_blockmask()
    lgf = lgf_ref[...]
    lgb = lgb_ref[...]
    kdec_f = jnp.exp((RET_CHUNK - 1.0 - c) * lgf)
    kdec_b = jnp.exp(c * lgb)
    cdec_f = jnp.exp(float(RET_CHUNK) * lgf)
    cdec_b = jnp.exp(float(RET_CHUNK) * lgb)

    def advance(state, chunk, kdec, cdec, out):
        rows = pl.ds(pl.multiple_of(chunk * RET_CHUNK, RET_CHUNK), RET_CHUNK)
        out[chunk] = state.astype(bf16)
        kd = (k_ref[rows, :].astype(f32) * kdec).astype(bf16)
        kv = lax.dot_general(kd, v_ref[rows, :], TN_DIMS, preferred_element_type=f32)
        return state * cdec + jnp.where(blockmask, kv, 0.0)

    def body(n, carry):
        sf, sb = carry
        return (advance(sf, n, kdec_f, cdec_f, sf_out), advance(sb, n_chunks - 1 - n, kdec_b, cdec_b, sb_out))

    sf, sb = lax.fori_loop(0, n_chunks, body, (s0f_ref[...], s0b_ref[...]), unroll=min(n_chunks, 8))
    sf_fin[...] = sf
    sb_fin[...] = sb


def retention_states(ret, lgf_lane, lgb_lane, s0f, s0b):
    L = ret.shape[0]
    N = L // RET_CHUNK
    col = lambda c: pl.BlockSpec((L, LANES), lambda hp: (0, c * N_PAIRS + hp))
    vec = pl.BlockSpec((1, LANES), lambda hp: (0, hp))
    st = pl.BlockSpec((None, LANES, LANES), lambda hp: (hp, 0, 0))
    per_chunk = pl.BlockSpec((N, None, LANES, LANES), lambda hp: (0, hp, 0, 0))
    st_shape = jax.ShapeDtypeStruct((N_PAIRS, LANES, LANES), f32)
    all_shape = jax.ShapeDtypeStruct((N, N_PAIRS, LANES, LANES), bf16)
    return pl.pallas_call(
        functools.partial(_ret_state_kernel, n_chunks=N),
        grid=(N_PAIRS,),
        in_specs=[col(1), col(2), vec, vec, st, st],
        out_specs=[per_chunk, per_chunk, st, st],
        out_shape=[all_shape, all_shape, st_shape, st_shape],
        compiler_params=_params("parallel"),
        name="retention_states",
    )(ret, ret, lgf_lane, lgb_lane, s0f, s0b)


def _ret_out_kernel(lgf_s, lgb_s, q_ref, k_ref, v_ref, gate_ref, sf_ref, sb_ref, lgf_ref, lgb_ref, gn_ref, o_ref,
                    *, cs):
    lo_half = _lo_half()
    diff = (lax.broadcasted_iota(jnp.int32, (RET_CHUNK, RET_CHUNK), 0)
            - lax.broadcasted_iota(jnp.int32, (RET_CHUNK, RET_CHUNK), 1)).astype(f32)
    c = lax.broadcasted_iota(jnp.int32, (RET_CHUNK, 1), 0).astype(f32)
    qdec_f = jnp.exp((c + 1.0) * lgf_ref[...])
    qdec_b = jnp.exp((RET_CHUNK - c) * lgb_ref[...])
    decay = [jnp.where(diff >= 0, jnp.exp(diff * lgf_s[h]), 0.0) + jnp.where(diff <= 0, jnp.exp(-diff * lgb_s[h]), 0.0)
             for h in range(RET_HEADS)]
    inv_hd = 1.0 / HEAD_DIM

    def head_mean(t):
        m0 = jnp.sum(jnp.where(lo_half, t, 0.0), axis=-1, keepdims=True) * inv_hd
        m1 = jnp.sum(jnp.where(lo_half, 0.0, t), axis=-1, keepdims=True) * inv_hd
        return jnp.where(lo_half, m0, m1)

    for ci in range(cs):
        rows = slice(RET_CHUNK * ci, RET_CHUNK * (ci + 1))
        q = q_ref[rows, :]
        qf32 = q.astype(f32)
        q_f = (qf32 * qdec_f).astype(bf16)
        q_b = (qf32 * qdec_b).astype(bf16)
        for hp in range(N_PAIRS):
            sl = slice(LANES * hp, LANES * (hp + 1))
            q2 = q[:, sl]
            k2 = k_ref[rows, sl]
            v2 = v_ref[rows, sl]
            lhs = []
            rhs = []
            for sub in range(2):
                hmask = lo_half if sub == 0 else jnp.logical_not(lo_half)
                qm = jnp.where(hmask, q2, jnp.zeros_like(q2))
                a = lax.dot_general(qm, k2, NT_DIMS, preferred_element_type=f32)
                lhs.append((a * decay[2 * hp + sub]).astype(bf16))
                rhs.append(jnp.where(hmask, v2, jnp.zeros_like(v2)))
            lhs += [q_f[:, sl], q_b[:, sl]]
            rhs += [sf_ref[ci, hp], sb_ref[ci, hp]]
            y = jnp.dot(jnp.concatenate(lhs, axis=1), jnp.concatenate(rhs, axis=0), preferred_element_type=f32)
            d = y - head_mean(y)
            yn = d * lax.rsqrt(head_mean(d * d) + EPS) * gn_ref[:, sl]
            o_ref[rows, sl] = (_silu(gate_ref[rows, sl]) * yn).astype(bf16)


def retention_outputs(ret, gate, sf, sb, lgf, lgb, lgf_lane, lgb_lane, gn):
    L = ret.shape[0]
    N = L // RET_CHUNK
    cs = min(RET_STEP, N)
    col = lambda c: pl.BlockSpec((cs * RET_CHUNK, RET_DIM), lambda n, *_: (n, c))
    vec = pl.BlockSpec((1, RET_DIM), lambda n, *_: (0, 0))
    st = pl.BlockSpec((cs, N_PAIRS, LANES, LANES), lambda n, *_: (n, 0, 0, 0))
    return pl.pallas_call(
        functools.partial(_ret_out_kernel, cs=cs),
        grid_spec=pltpu.PrefetchScalarGridSpec(
            num_scalar_prefetch=2,
            grid=(N // cs,),
            in_specs=[col(0), col(1), col(2), col(0), st, st, vec, vec, vec],
            out_specs=pl.BlockSpec((cs * RET_CHUNK, RET_DIM), lambda n, *_: (n, 0)),
        ),
        out_shape=jax.ShapeDtypeStruct((L, RET_DIM), bf16),
        compiler_params=_params("parallel"),
        name="retention_outputs",
    )(lgf, lgb, ret, ret, ret, gate, sf, sb, lgf_lane, lgb_lane, gn)


FFN_HALO = 16
FFN_TF = 256


def _out_ffn_kernel(hp_ref, h_ref, hn_ref, pp_ref, p_ref, pn_ref, np_ref, n_ref, nn_ref, rp_ref, r_ref, rn_ref,
                    mod_ref, g_ref, wo_ref, wu_ref, cw_ref, cb_ref, wd_ref, fg_ref, o_ref, aext, gbuf, *, tm, final):
    i = pl.program_id(0)
    ni = pl.num_programs(0)
    ext = tm + 2 * FFN_HALO

    def rows3(prev, cur, nxt):
        return jnp.concatenate([prev[...], cur[...], nxt[...]], axis=0)

    mixed = jnp.concatenate([rows3(pp_ref, p_ref, pn_ref), rows3(np_ref, n_ref, nn_ref),
                             rows3(rp_ref, r_ref, rn_ref)], axis=1)
    hx = rows3(hp_ref, h_ref, hn_ref) + mod_ref[2:3, :] * jnp.dot(mixed, wo_ref[...], preferred_element_type=f32)
    o_ref[...] = hx[FFN_HALO:FFN_HALO + tm]
    a = hx * lax.rsqrt(jnp.mean(hx * hx, axis=-1, keepdims=True) + EPS) * g_ref[...]
    a = a * (1.0 + mod_ref[4:5, :]) + mod_ref[3:4, :]
    row = lax.broadcasted_iota(jnp.int32, (ext, 1), 0)
    inside = jnp.logical_and(jnp.logical_or(row >= FFN_HALO, i > 0), jnp.logical_or(row < FFN_HALO + tm, i < ni - 1))
    aext[...] = jnp.where(inside, a, 0.0).astype(bf16)
    ae = aext[...]

    def branch(lo):
        u = jnp.dot(ae, wu_ref[:, lo:lo + FFN_TF], preferred_element_type=f32)
        up = pltpu.roll(u, 1, 0)[FFN_HALO:FFN_HALO + tm]
        un = pltpu.roll(u, ext - 1, 0)[FFN_HALO:FFN_HALO + tm]
        uc = u[FFN_HALO:FFN_HALO + tm]
        cw = cw_ref[:, lo:lo + FFN_TF]
        return up * cw[0:1, :] + uc * cw[1:2, :] + un * cw[2:3, :] + cb_ref[:, lo:lo + FFN_TF]

    for s in range(D_FF // FFN_TF):
        lo = FFN_TF * s
        gbuf[:, lo:lo + FFN_TF] = (_silu(branch(lo)) * branch(D_FF + lo)).astype(bf16)
    y = jnp.dot(gbuf[...], wd_ref[...], preferred_element_type=f32)
    out = o_ref[...] + mod_ref[5:6, :] * y
    if final:
        out = out * lax.rsqrt(jnp.mean(out * out, axis=-1, keepdims=True) + EPS) * fg_ref[...]
    o_ref[...] = out


def out_proj_ffn(h, pool_x, na_x, ret_x, mod, g2, w_out, w_up, conv_w, conv_b, w_down, layer, final_g, final):
    L = h.shape[0]
    tm = _row_tile(L)
    hb = tm // FFN_HALO

    def halo3(n):
        return [
            pl.BlockSpec((FFN_HALO, n), lambda i: (jnp.maximum(i * hb - 1, 0), 0)),
            pl.BlockSpec((tm, n), lambda i: (i, 0)),
            pl.BlockSpec((FFN_HALO, n), lambda i: (jnp.minimum((i + 1) * hb, L // FFN_HALO - 1), 0)),
        ]

    once = dict(pipeline_mode=pl.Buffered(1))
    const = lambda arr: pl.BlockSpec(arr.shape, lambda i: (0,) * arr.ndim, **once)
    stack = lambda arr: _layer_spec(arr, layer, **once)
    consts = (mod, g2, w_out, w_up, conv_w, conv_b, w_down, final_g)
    const_specs = [const(mod), const(g2), stack(w_out), stack(w_up), const(conv_w), const(conv_b), stack(w_down),
                   const(final_g)]
    return pl.pallas_call(
        functools.partial(_out_ffn_kernel, tm=tm, final=final),
        grid=(L // tm,),
        in_specs=halo3(D_MODEL) + halo3(POOL_DIM) + halo3(NA_DIM) + halo3(RET_DIM) + const_specs,
        out_specs=pl.BlockSpec((tm, D_MODEL), lambda i: (i, 0)),
        out_shape=jax.ShapeDtypeStruct((L, D_MODEL), f32),
        scratch_shapes=[pltpu.VMEM((tm + 2 * FFN_HALO, D_MODEL), bf16), pltpu.VMEM((tm, D_FF), bf16)],
        compiler_params=_params("parallel"),
        name="out_proj_ffn",
    )(h, h, h, pool_x, pool_x, pool_x, na_x, na_x, na_x, ret_x, ret_x, ret_x, *consts)


def kernel(x, c, ctx, c_ctx, w_mod, b_mod, norm1_g, w_in, pool_w, pool_scale, na_rpb, ret_decay_fwd, ret_decay_bwd,
           ret_gn_g, w_out, norm2_g, w_up, conv_w, conv_b, w_down, final_g):
    B, L, _ = x.shape
    assert B == 1
    n_ctx = ctx.shape[1]
    h = x[0]
    hc = ctx[0]
    mods = mod_vectors(c, c_ctx, w_mod, b_mod)
    rope = rope_tables(L // GRID_W)
    no_rope = rope_identity(n_ctx // GRID_W)
    zero_state = jnp.zeros((N_PAIRS, LANES, LANES), f32)
    fg = final_g.reshape(1, D_MODEL)
    wi = w_in.astype(bf16)
    wo = w_out.astype(bf16)
    wu = w_up.astype(bf16)
    wd = w_down.astype(bf16)
    bias_rows = na_bias_rows(na_rpb)
    for l in range(DEPTH):
        need_ctx = l < DEPTH - 1
        mx = mods[l, 0].reshape(6, D_MODEL)
        mc = mods[l, 1].reshape(6, D_MODEL)
        g1 = norm1_g[l].reshape(1, D_MODEL)
        g2 = norm2_g[l].reshape(1, D_MODEL)
        cb = conv_b[l].reshape(1, 2 * D_FF)
        wp = pool_blockdiag(pool_w[l])
        ps = pool_scale[l].reshape(1, POOL_DIM)
        lgf = ret_decay_fwd[l].astype(f32)
        lgb = ret_decay_bwd[l].astype(f32)
        lgf_lane = jnp.repeat(lgf, HEAD_DIM).reshape(1, RET_DIM)
        lgb_lane = jnp.repeat(lgb, HEAD_DIM).reshape(1, RET_DIM)
        gn = ret_gn_g[l].reshape(1, RET_DIM)
        ffn_w = (g2, wo, wu, conv_w[l], cb, wd, l, fg)

        pc, nac, retc, gatec = in_proj(hc, g1, mc, wi, l, no_rope)
        px, nax, retx, gatex = in_proj(h, g1, mx, wi, l, rope)

        sfc, sbc, sf0, sb0 = retention_states(retc, lgf_lane, lgb_lane, zero_state, zero_state)
        sfx, sbx, _, _ = retention_states(retx, lgf_lane, lgb_lane, sf0, sb0)

        pool_x = pool_mixer(px, wp, ps)
        na_x = neighborhood_attention(nax, nac, bias_rows[l])
        ret_x = retention_outputs(retx, gatex, sfx, sbx, lgf, lgb, lgf_lane, lgb_lane, gn)
        h = out_proj_ffn(h, pool_x, na_x, ret_x, mx, *ffn_w, final=not need_ctx)

        if need_ctx:
            pool_c = pool_mixer(pc, wp, ps)
            na_c = context_attention(nac)
            ret_c = retention_outputs(retc, gatec, sfc, sbc, lgf, lgb, lgf_lane, lgb_lane, gn)
            hc = out_proj_ffn(hc, pool_c, na_c, ret_c, mc, *ffn_w, final=False)
    return h[None]
```

```python
import functools

import numpy as np
import jax
import jax.numpy as jnp
from jax import lax
from jax.experimental import pallas as pl
from jax.experimental.pallas import tpu as pltpu

f32 = jnp.float32
bf16 = jnp.bfloat16

D_MODEL = 1024
DEPTH = 4
GRID_W = 64
HEAD_DIM = 64
POOL_GROUPS = 4
POOL_WINDOWS = (2, 4, 8, 16)
POOL_DIM = 256
NA_HEADS = 6
NA_DIM = 384
NA_KH = 8
NA_KW = 16
RET_HEADS = 6
RET_DIM = 384
RET_CHUNK = 128
D_FF = 2816
ROPE_BASE = 10000.0
ROT_QUARTER = HEAD_DIM // 4
EPS = 1e-6
NEG_INF = -1e30
IN_DIM = POOL_DIM + 3 * NA_DIM + 4 * RET_DIM
OFF_NA = POOL_DIM
OFF_RET = OFF_NA + 3 * NA_DIM
OFF_GATE = OFF_RET + 3 * RET_DIM

LANES = 128
N_PAIRS = NA_HEADS // 2
VMEM_LIMIT = 56 * 1024 * 1024

LOG2E = float(np.log2(np.e))
NT_DIMS = (((1,), (1,)), ((), ()))
TN_DIMS = (((0,), (0,)), ((), ()))


def _params(*sem):
    return pltpu.CompilerParams(dimension_semantics=sem, vmem_limit_bytes=VMEM_LIMIT)


def _row_tile(L):
    return min(L, 512)


def _silu(x):
    return x * jax.nn.sigmoid(x)


def _lo_half():
    return lax.broadcasted_iota(jnp.int32, (1, LANES), 1) < HEAD_DIM


def _mod_kernel(s_ref, w_ref, b_ref, o_ref):
    a = _silu(s_ref[...]).astype(bf16)
    o_ref[...] = jnp.dot(a, w_ref[...].astype(bf16), preferred_element_type=f32) + b_ref[...]


def mod_vectors(c, c_ctx, w_mod, b_mod):
    s = jnp.zeros((8, D_MODEL), f32).at[0].set(c[0]).at[1].set(c_ctx)
    return pl.pallas_call(
        _mod_kernel,
        grid=(DEPTH, 6),
        in_specs=[
            pl.BlockSpec((8, D_MODEL), lambda l, j: (0, 0)),
            pl.BlockSpec((None, D_MODEL, D_MODEL), lambda l, j: (l, 0, j)),
            pl.BlockSpec((None, 1, D_MODEL), lambda l, j: (l, 0, j)),
        ],
        out_specs=pl.BlockSpec((None, 8, D_MODEL), lambda l, j: (l, 0, j)),
        out_shape=jax.ShapeDtypeStruct((DEPTH, 8, 6 * D_MODEL), f32),
        compiler_params=_params("parallel", "parallel"),
        name="mod_vectors",
    )(s, w_mod, b_mod.reshape(DEPTH, 1, 6 * D_MODEL))


def _in_proj_kernel(h_ref, g_ref, mod_ref, w_ref, cos_row_ref, sin_row_ref, cos_col_ref, sin_col_ref, *refs):
    n_cast = (len(refs) - 4) // 2
    cast_in = refs[:n_cast]
    p_ref, na_ref, ret_ref, gate_ref = refs[n_cast:n_cast + 4]
    for src, dst in zip(cast_in, refs[n_cast + 4:]):
        dst[...] = src[...].astype(bf16)
    h = h_ref[...]
    y = h * lax.rsqrt(jnp.mean(h * h, axis=-1, keepdims=True) + EPS) * g_ref[...]
    a = (y * (1.0 + mod_ref[1:2, :]) + mod_ref[0:1, :]).astype(bf16)

    def proj(lo, n):
        return jnp.dot(a, w_ref[:, lo:lo + n], preferred_element_type=f32)

    p_ref[...] = proj(0, POOL_DIM)
    na = proj(OFF_NA, 3 * NA_DIM)
    na_ref[:, 0:NA_DIM] = (na[:, 0:NA_DIM] * (HEAD_DIM ** -0.5 * LOG2E)).astype(bf16)
    na_ref[:, NA_DIM:] = na[:, NA_DIM:].astype(bf16)
    lane = lax.broadcasted_iota(jnp.int32, (1, LANES), 1)
    first = lane % (2 * ROT_QUARTER) < ROT_QUARTER
    row_lane = lane % HEAD_DIM < HEAD_DIM // 2

    def table(row_ref, col_ref):
        col = col_ref[...]
        return jnp.concatenate(
            [jnp.where(row_lane, jnp.broadcast_to(row_ref[j:j + 1, :], col.shape), col)
             for j in range(row_ref.shape[0])], axis=0)

    cos = table(cos_row_ref, cos_col_ref)
    sin = table(sin_row_ref, sin_col_ref)
    qk = proj(OFF_RET, 2 * RET_DIM)
    for slab in range(2 * RET_DIM // LANES):
        lo = LANES * slab
        x = qk[:, lo:lo + LANES]
        partner = jnp.where(first, pltpu.roll(x, LANES - ROT_QUARTER, 1), pltpu.roll(x, ROT_QUARTER, 1))
        x = x * cos + partner * sin
        if lo >= RET_DIM:
            x = x * HEAD_DIM ** -0.5
        ret_ref[:, lo:lo + LANES] = x.astype(bf16)
    vg = proj(OFF_RET + 2 * RET_DIM, 2 * RET_DIM)
    ret_ref[:, 2 * RET_DIM:] = vg[:, :RET_DIM].astype(bf16)
    gate_ref[...] = vg[:, RET_DIM:]


def in_proj(h, g, mod, w_in, rope, cast=()):
    L = h.shape[0]
    tm = _row_tile(L)
    steps = L // tm
    assert tm % GRID_W == 0
    cast_in, cast_out, cast_shapes = [], [], []
    for stack, layer in cast:
        _, k, n = stack.shape
        assert k % (16 * steps) == 0
        cast_in.append(pl.BlockSpec((None, k // steps, n), functools.partial(lambda i, layer: (layer, i, 0), layer=layer)))
        cast_out.append(pl.BlockSpec((k // steps, n), lambda i: (i, 0)))
        cast_shapes.append(jax.ShapeDtypeStruct((k, n), bf16))
    row = lambda n: pl.BlockSpec((tm, n), lambda i: (i, 0))
    full = lambda a: pl.BlockSpec(a.shape, lambda i: (0,) * a.ndim)
    grid_rows = pl.BlockSpec((tm // GRID_W, LANES), lambda i: (i, 0))
    grid_cols = pl.BlockSpec((GRID_W, LANES), lambda i: (0, 0))
    return pl.pallas_call(
        _in_proj_kernel,
        grid=(L // tm,),
        in_specs=[row(D_MODEL), full(g), full(mod), full(w_in),
                  grid_rows, grid_rows, grid_cols, grid_cols] + cast_in,
        out_specs=[row(POOL_DIM), row(3 * NA_DIM), row(3 * RET_DIM), row(RET_DIM)] + cast_out,
        out_shape=[
            jax.ShapeDtypeStruct((L, POOL_DIM), f32),
            jax.ShapeDtypeStruct((L, 3 * NA_DIM), bf16),
            jax.ShapeDtypeStruct((L, 3 * RET_DIM), bf16),
            jax.ShapeDtypeStruct((L, RET_DIM), f32),
        ] + cast_shapes,
        compiler_params=_params("parallel"),
        name="in_proj",
    )(h, g, mod, w_in, *rope, *[stack for stack, _ in cast])


def rope_tables(rows):
    inv = ROPE_BASE ** (-jnp.arange(ROT_QUARTER, dtype=f32) / ROT_QUARTER)

    def factors(n):
        ang = jnp.arange(n, dtype=f32)[:, None] * inv[None, :]
        cos = jnp.tile(jnp.cos(ang), (1, LANES // ROT_QUARTER))
        sin = jnp.tile(jnp.concatenate([-jnp.sin(ang), jnp.sin(ang)], axis=-1), (1, LANES // (2 * ROT_QUARTER)))
        return cos, sin

    return factors(rows) + factors(GRID_W)


def rope_identity(rows):
    ones = lambda n: jnp.ones((n, LANES), f32)
    zeros = lambda n: jnp.zeros((n, LANES), f32)
    return ones(rows), zeros(rows), ones(GRID_W), zeros(GRID_W)


POOL_HALO = 8
POOL_MARGIN = 8


def _pooled_rows(prev, x_ref, nxt, w_ref, scale_ref, lvl_ref, *, first, last, row0, L):
    assert POOL_WINDOWS == (2, 4, 8, 16)
    tm = x_ref.shape[0]
    halo = prev.shape[0]
    data = tm + 2 * halo
    m0 = POOL_MARGIN
    zeros = jnp.zeros((m0, POOL_DIM), f32)
    for k in range(3):
        lvl_ref[k, 0:m0, :] = zeros
        lvl_ref[k, m0 + data:, :] = zeros
    lvl_ref[0, m0:m0 + halo, :] = jnp.where(first, 0.0, prev)
    lvl_ref[0, m0 + halo:m0 + halo + tm, :] = x_ref[...]
    lvl_ref[0, m0 + halo + tm:m0 + data, :] = jnp.where(last, 0.0, nxt)

    def shifted(k, d, rows=data, start=m0):
        return lvl_ref[k, start + d:start + d + rows, :]

    lvl_ref[1, m0:m0 + data, :] = shifted(0, -1) + shifted(0, 0)
    lvl_ref[2, m0:m0 + data, :] = shifted(1, -1) + shifted(1, 1)
    lvl_ref[3, m0:m0 + data, :] = shifted(2, -2) + shifted(2, 2)
    out_rows = data - 2 * POOL_HALO
    centre = m0 + POOL_HALO
    x = shifted(0, 0, out_rows, centre)
    sums = [shifted(k, 0, out_rows, centre) for k in (1, 2, 3)]
    sums.append(shifted(3, -4, out_rows, centre) + shifted(3, 4, out_rows, centre))
    lane = lax.broadcasted_iota(jnp.int32, (1, POOL_DIM), 1)
    gdim = POOL_DIM // POOL_GROUPS
    total = sums[-1]
    half = jnp.full((1, POOL_DIM), POOL_WINDOWS[-1] // 2, jnp.int32)
    for gi in range(POOL_GROUPS - 2, -1, -1):
        sel = lane < gdim * (gi + 1)
        total = jnp.where(sel, sums[gi], total)
        half = jnp.where(sel, POOL_WINDOWS[gi] // 2, half)
    t = row0 - (halo - POOL_HALO) + lax.broadcasted_iota(jnp.int32, (out_rows, 1), 0)
    count = jnp.maximum(jnp.minimum(t + half, L) - jnp.maximum(t - half, 0), 1)
    m = (total / count.astype(f32) - x).astype(bf16)
    return (jnp.dot(m, w_ref[...], preferred_element_type=f32) * scale_ref[...]).astype(bf16)


def pool_blockdiag(pool_w):
    gdim = POOL_DIM // POOL_GROUPS
    w = jnp.zeros((POOL_DIM, POOL_DIM), f32)
    for g in range(POOL_GROUPS):
        w = w.at[g * gdim:(g + 1) * gdim, g * gdim:(g + 1) * gdim].set(pool_w[g])
    return w.astype(bf16)


NA_RB = 4
NA_QT = NA_RB * GRID_W
NA_WIN = NA_RB + NA_KH
NA_KB = NA_WIN // NA_RB
NA_DR_MASKED = 2 * NA_KH - 1
NA_VARIANTS = 3


def na_bias_rows(rpb):
    assert LANES == 2 * GRID_W
    w = jnp.zeros(rpb.shape[:-1] + (LANES,), f32)
    w = w.at[..., :NA_KW].set(rpb[..., NA_KW - 1:])
    w = w.at[..., LANES - (NA_KW - 1):].set(rpb[..., :NA_KW - 1])
    return w * LOG2E


def _na_fill_bias(rows_ref, bias_scr, dr_maps):
    lane = lax.broadcasted_iota(jnp.int32, (GRID_W, LANES), 1)
    qcol = lax.broadcasted_iota(jnp.int32, (GRID_W, LANES), 0)
    kcol = lane % GRID_W
    ws = jnp.clip(qcol - NA_KW // 2, 0, GRID_W - NA_KW)
    in_window = jnp.logical_and(kcol >= ws, kcol < ws + NA_KW)
    masked = jnp.full((GRID_W, LANES), NEG_INF, f32)
    for h in range(NA_HEADS):
        for dr in range(NA_DR_MASKED + 1):
            if dr == NA_DR_MASKED:
                tile = masked
            else:
                row = jnp.broadcast_to(rows_ref[h, dr:dr + 1, :], (GRID_W, LANES))
                t = pltpu.roll(row, 0, 1, stride=1, stride_axis=0)
                t = jnp.where(lane < GRID_W, t, pltpu.roll(t, GRID_W, 1))
                tile = jnp.where(in_window, t, NEG_INF)
            for var in range(NA_VARIANTS):
                for j, i in zip(*np.nonzero(dr_maps[var] == dr)):
                    half = GRID_W * (int(i) % 2)
                    bias_scr[var, h, GRID_W * j:GRID_W * (j + 1), GRID_W * i:GRID_W * (i + 1)] = (
                        tile[:, half:half + GRID_W])


def _na_key_base(g, nb):
    return jnp.clip(g - 1, 0, nb - NA_KB)


def _na_dr_maps(rows):
    nb = rows // NA_RB
    maps = []
    for g in range(nb):
        kb = NA_RB * min(max(g - 1, 0), nb - NA_KB)
        m = np.full((NA_RB, NA_WIN), NA_DR_MASKED)
        for j in range(NA_RB):
            r = NA_RB * g + j
            rs = min(max(r - NA_KH // 2, 0), rows - NA_KH)
            for i in range(rs - kb, rs - kb + NA_KH):
                m[j, i] = kb + i - r + NA_KH - 1
        maps.append(m)
    variants = [maps[0], maps[1], maps[nb - 1]]
    for g in range(nb):
        assert (maps[g] == variants[0 if g == 0 else 2 if g == nb - 1 else 1]).all()
    return variants


def _per_head_rows(x, lo_half):
    zero = jnp.zeros_like(x)
    return jnp.concatenate([jnp.where(lo_half, x, zero), jnp.where(lo_half, zero, x)], axis=0)


def _with_ones(v, lo_half, sub):
    ones = jnp.ones_like(v)
    return jnp.where(lo_half, v, ones) if sub == 0 else jnp.where(lo_half, ones, v)


def _softmax_pv(s_parts, v_parts):
    maxes = [jnp.max(s, axis=-1, keepdims=True) for s in s_parts]
    m = functools.reduce(jnp.maximum, maxes)
    out = None
    for s, v, mk in zip(s_parts, v_parts, maxes):
        pv = jnp.dot(jnp.exp2(s - mk).astype(bf16), v, preferred_element_type=f32) * jnp.exp2(mk - m)
        out = pv if out is None else out + pv
    return out / pltpu.roll(out, HEAD_DIM, 1)


def _na_kernel(q_ref, k0_ref, k1_ref, k2_ref, v0_ref, v1_ref, v2_ref, kc_ref, vc_ref, rows_ref, o_ref,
               bias_scr, *, nb, dr_maps):
    g = pl.program_id(0)

    @pl.when(g == 0)
    def _():
        _na_fill_bias(rows_ref, bias_scr, dr_maps)

    var = jnp.where(g == 0, 0, jnp.where(g == nb - 1, 2, 1))
    lo_half = _lo_half()
    k_refs = (k0_ref, k1_ref, k2_ref)
    v_refs = (v0_ref, v1_ref, v2_ref)
    for hp in range(N_PAIRS):
        sl = slice(LANES * hp, LANES * (hp + 1))
        q2 = q_ref[:, sl]
        values = [vr[:, sl] for vr in v_refs] + [vc_ref[:, sl]]
        outs = []
        for sub in range(2):
            values_sub = [_with_ones(v, lo_half, sub) for v in values]
            qm = jnp.where(lo_half if sub == 0 else jnp.logical_not(lo_half), q2, jnp.zeros_like(q2))
            scores = []
            for kb in range(NA_KB):
                s = lax.dot_general(qm, k_refs[kb][:, sl], NT_DIMS, preferred_element_type=f32)
                tb = bias_scr[var, 2 * hp + sub, :, NA_QT * kb:NA_QT * (kb + 1)]
                scores.append(jnp.where(tb < 0.5 * NEG_INF, NEG_INF, s + tb))
            scores.append(lax.dot_general(qm, kc_ref[:, sl], NT_DIMS, preferred_element_type=f32))
            outs.append(_softmax_pv(scores, values_sub))
        o_ref[:, sl] = jnp.where(lo_half, outs[0], outs[1]).astype(bf16)


def neighborhood_attention(na, na_ctx, bias_rows):
    L = na.shape[0]
    rows = L // GRID_W
    nb = rows // NA_RB
    assert rows % NA_RB == 0 and nb > NA_KB
    ctx = na_ctx.shape[0]
    blk = lambda off, col: pl.BlockSpec((NA_QT, NA_DIM), lambda g: (_na_key_base(g, nb) + off, col))
    return pl.pallas_call(
        functools.partial(_na_kernel, nb=nb, dr_maps=_na_dr_maps(rows)),
        grid=(nb,),
        in_specs=[
            pl.BlockSpec((NA_QT, NA_DIM), lambda g: (g, 0)),
            blk(0, 1), blk(1, 1), blk(2, 1),
            blk(0, 2), blk(1, 2), blk(2, 2),
            pl.BlockSpec((ctx, NA_DIM), lambda g: (0, 1)),
            pl.BlockSpec((ctx, NA_DIM), lambda g: (0, 2)),
            pl.BlockSpec(bias_rows.shape, lambda g: (0, 0, 0)),
        ],
        out_specs=pl.BlockSpec((NA_QT, NA_DIM), lambda g: (g, 0)),
        out_shape=jax.ShapeDtypeStruct((L, NA_DIM), bf16),
        scratch_shapes=[pltpu.VMEM((NA_VARIANTS, NA_HEADS, NA_QT, NA_WIN * GRID_W), f32)],
        compiler_params=_params("arbitrary"),
        name="neighborhood_attention",
    )(na, na, na, na, na, na, na, na_ctx, na_ctx, bias_rows)


def _ctx_attn_kernel(q_ref, k_ref, v_ref, o_ref):
    lo_half = _lo_half()
    for hp in range(N_PAIRS):
        sl = slice(LANES * hp, LANES * (hp + 1))
        q2 = q_ref[:, sl]
        k2 = k_ref[:, sl]
        v2 = v_ref[:, sl]
        outs = []
        for sub in range(2):
            qm = jnp.where(lo_half if sub == 0 else jnp.logical_not(lo_half), q2, jnp.zeros_like(q2))
            s = lax.dot_general(qm, k2, NT_DIMS, preferred_element_type=f32)
            outs.append(_softmax_pv([s], [_with_ones(v2, lo_half, sub)]))
        o_ref[:, sl] = jnp.where(lo_half, outs[0], outs[1]).astype(bf16)


def context_attention(na_ctx):
    ctx = na_ctx.shape[0]
    col = lambda c: pl.BlockSpec((ctx, NA_DIM), lambda i: (0, c))
    return pl.pallas_call(
        _ctx_attn_kernel,
        grid=(1,),
        in_specs=[col(0), col(1), col(2)],
        out_specs=pl.BlockSpec((ctx, NA_DIM), lambda i: (0, 0)),
        out_shape=jax.ShapeDtypeStruct((ctx, NA_DIM), bf16),
        compiler_params=_params("arbitrary"),
        name="context_attention",
    )(na_ctx, na_ctx, na_ctx)


RET_STEP = 4


def _pair_blockmask():
    r = lax.broadcasted_iota(jnp.int32, (LANES, LANES), 0) < HEAD_DIM
    c = lax.broadcasted_iota(jnp.int32, (LANES, LANES), 1) < HEAD_DIM
    return r == c


def _ret_state_kernel(k_ref, v_ref, lgf_ref, lgb_ref, s0f_ref, s0b_ref, sf_out, sb_out, sf_fin, sb_fin, *, n_chunks):
    c = lax.broadcasted_iota(jnp.int32, (RET_CHUNK, 1), 0).astype(f32)
    blockmask = _pair_blockmask()
    lgf = lgf_ref[...]
    lgb = lgb_ref[...]
    kdec_f = jnp.exp((RET_CHUNK - 1.0 - c) * lgf)
    kdec_b = jnp.exp(c * lgb)
    cdec_f = jnp.exp(float(RET_CHUNK) * lgf)
    cdec_b = jnp.exp(float(RET_CHUNK) * lgb)

    def advance(state, chunk, kdec, cdec, out):
        rows = pl.ds(pl.multiple_of(chunk * RET_CHUNK, RET_CHUNK), RET_CHUNK)
        out[chunk] = state.astype(bf16)
        kd = (k_ref[rows, :].astype(f32) * kdec).astype(bf16)
        kv = lax.dot_general(kd, v_ref[rows, :], TN_DIMS, preferred_element_type=f32)
        return state * cdec + jnp.where(blockmask, kv, 0.0)

    def body(n, carry):
        sf, sb = carry
        return (advance(sf, n, kdec_f, cdec_f, sf_out), advance(sb, n_chunks - 1 - n, kdec_b, cdec_b, sb_out))

    sf, sb = lax.fori_loop(0, n_chunks, body, (s0f_ref[...], s0b_ref[...]), unroll=min(n_chunks, 8))
    sf_fin[...] = sf
    sb_fin[...] = sb


def retention_states(ret, lgf_lane, lgb_lane, s0f, s0b):
    L = ret.shape[0]
    N = L // RET_CHUNK
    col = lambda c: pl.BlockSpec((L, LANES), lambda hp: (0, c * N_PAIRS + hp))
    vec = pl.BlockSpec((1, LANES), lambda hp: (0, hp))
    st = pl.BlockSpec((None, LANES, LANES), lambda hp: (hp, 0, 0))
    per_chunk = pl.BlockSpec((N, None, LANES, LANES), lambda hp: (0, hp, 0, 0))
    st_shape = jax.ShapeDtypeStruct((N_PAIRS, LANES, LANES), f32)
    all_shape = jax.ShapeDtypeStruct((N, N_PAIRS, LANES, LANES), bf16)
    return pl.pallas_call(
        functools.partial(_ret_state_kernel, n_chunks=N),
        grid=(N_PAIRS,),
        in_specs=[col(1), col(2), vec, vec, st, st],
        out_specs=[per_chunk, per_chunk, st, st],
        out_shape=[all_shape, all_shape, st_shape, st_shape],
        compiler_params=_params("parallel"),
        name="retention_states",
    )(ret, ret, lgf_lane, lgb_lane, s0f, s0b)


def _ret_out_kernel(lgf_s, lgb_s, q_ref, k_ref, v_ref, gate_ref, sf_ref, sb_ref, lgf_ref, lgb_ref, gn_ref, o_ref,
                    *, cs):
    lo_half = _lo_half()
    diff = (lax.broadcasted_iota(jnp.int32, (RET_CHUNK, RET_CHUNK), 0)
            - lax.broadcasted_iota(jnp.int32, (RET_CHUNK, RET_CHUNK), 1)).astype(f32)
    c = lax.broadcasted_iota(jnp.int32, (RET_CHUNK, 1), 0).astype(f32)
    qdec_f = jnp.exp((c + 1.0) * lgf_ref[...])
    qdec_b = jnp.exp((RET_CHUNK - c) * lgb_ref[...])
    decay = [jnp.where(diff >= 0, jnp.exp(diff * lgf_s[h]), 0.0) + jnp.where(diff <= 0, jnp.exp(-diff * lgb_s[h]), 0.0)
             for h in range(RET_HEADS)]
    inv_hd = 1.0 / HEAD_DIM

    def head_mean(t):
        m0 = jnp.sum(jnp.where(lo_half, t, 0.0), axis=-1, keepdims=True) * inv_hd
        m1 = jnp.sum(jnp.where(lo_half, 0.0, t), axis=-1, keepdims=True) * inv_hd
        return jnp.where(lo_half, m0, m1)

    for ci in range(cs):
        rows = slice(RET_CHUNK * ci, RET_CHUNK * (ci + 1))
        q = q_ref[rows, :]
        qf32 = q.astype(f32)
        q_f = (qf32 * qdec_f).astype(bf16)
        q_b = (qf32 * qdec_b).astype(bf16)
        for hp in range(N_PAIRS):
            sl = slice(LANES * hp, LANES * (hp + 1))
            q2 = q[:, sl]
            k2 = k_ref[rows, sl]
            v2 = v_ref[rows, sl]
            a = lax.dot_general(q2, _per_head_rows(k2, lo_half), NT_DIMS, preferred_element_type=f32)
            lhs = [(a[:, :RET_CHUNK] * decay[2 * hp]).astype(bf16), (a[:, RET_CHUNK:] * decay[2 * hp + 1]).astype(bf16),
                   q_f[:, sl], q_b[:, sl]]
            rhs = [_per_head_rows(v2, lo_half), sf_ref[ci, hp], sb_ref[ci, hp]]
            y = jnp.dot(jnp.concatenate(lhs, axis=1), jnp.concatenate(rhs, axis=0), preferred_element_type=f32)
            d = y - head_mean(y)
            yn = d * lax.rsqrt(head_mean(d * d) + EPS) * gn_ref[:, sl]
            o_ref[rows, sl] = (_silu(gate_ref[rows, sl]) * yn).astype(bf16)


def retention_outputs(ret, gate, sf, sb, lgf, lgb, lgf_lane, lgb_lane, gn):
    L = ret.shape[0]
    N = L // RET_CHUNK
    cs = min(RET_STEP, N)
    col = lambda c: pl.BlockSpec((cs * RET_CHUNK, RET_DIM), lambda n, *_: (n, c))
    vec = pl.BlockSpec((1, RET_DIM), lambda n, *_: (0, 0))
    st = pl.BlockSpec((cs, N_PAIRS, LANES, LANES), lambda n, *_: (n, 0, 0, 0))
    return pl.pallas_call(
        functools.partial(_ret_out_kernel, cs=cs),
        grid_spec=pltpu.PrefetchScalarGridSpec(
            num_scalar_prefetch=2,
            grid=(N // cs,),
            in_specs=[col(0), col(1), col(2), col(0), st, st, vec, vec, vec],
            out_specs=pl.BlockSpec((cs * RET_CHUNK, RET_DIM), lambda n, *_: (n, 0)),
        ),
        out_shape=jax.ShapeDtypeStruct((L, RET_DIM), bf16),
        compiler_params=_params("parallel"),
        name="retention_outputs",
    )(lgf, lgb, ret, ret, ret, gate, sf, sb, lgf_lane, lgb_lane, gn)


FFN_HALO = 16
FFN_TF = 256
POOL_IN_HALO = FFN_HALO + POOL_HALO
POOL_IN_BLOCK = 2 * FFN_HALO


def _out_ffn_kernel(hp_ref, h_ref, hn_ref, pp_ref, p_ref, pn_ref, np_ref, n_ref, nn_ref, rp_ref, r_ref, rn_ref,
                    mod_ref, g_ref, wp_ref, ps_ref, wo_ref, wu_ref, cw_ref, cb_ref, wd_ref, fg_ref, o_ref,
                    aext, gbuf, lvl, *, tm, L, final):
    i = pl.program_id(0)
    ni = pl.num_programs(0)
    ext = tm + 2 * FFN_HALO

    def rows3(prev, cur, nxt):
        return jnp.concatenate([prev[...], cur[...], nxt[...]], axis=0)

    y = jnp.dot(jnp.concatenate([rows3(np_ref, n_ref, nn_ref), rows3(rp_ref, r_ref, rn_ref)], axis=1),
                wo_ref[POOL_DIM:, :], preferred_element_type=f32)
    pool_x = _pooled_rows(pp_ref[POOL_IN_BLOCK - POOL_IN_HALO:, :], p_ref, pn_ref[:POOL_IN_HALO, :], wp_ref, ps_ref,
                          lvl, first=i == 0, last=i == ni - 1, row0=i * tm, L=L)
    y = y + jnp.dot(pool_x, wo_ref[:POOL_DIM, :], preferred_element_type=f32)
    hx = rows3(hp_ref, h_ref, hn_ref) + mod_ref[2:3, :] * y
    o_ref[...] = hx[FFN_HALO:FFN_HALO + tm]
    a = hx * lax.rsqrt(jnp.mean(hx * hx, axis=-1, keepdims=True) + EPS) * g_ref[...]
    a = a * (1.0 + mod_ref[4:5, :]) + mod_ref[3:4, :]
    row = lax.broadcasted_iota(jnp.int32, (ext, 1), 0)
    inside = jnp.logical_and(jnp.logical_or(row >= FFN_HALO, i > 0), jnp.logical_or(row < FFN_HALO + tm, i < ni - 1))
    aext[...] = jnp.where(inside, a, 0.0).astype(bf16)
    ae = aext[...]

    def branch(lo):
        u = jnp.dot(ae, wu_ref[:, lo:lo + FFN_TF], preferred_element_type=f32)
        up = pltpu.roll(u, 1, 0)[FFN_HALO:FFN_HALO + tm]
        un = pltpu.roll(u, ext - 1, 0)[FFN_HALO:FFN_HALO + tm]
        uc = u[FFN_HALO:FFN_HALO + tm]
        cw = cw_ref[:, lo:lo + FFN_TF]
        return up * cw[0:1, :] + uc * cw[1:2, :] + un * cw[2:3, :] + cb_ref[:, lo:lo + FFN_TF]

    for s in range(D_FF // FFN_TF):
        lo = FFN_TF * s
        gbuf[:, lo:lo + FFN_TF] = (_silu(branch(lo)) * branch(D_FF + lo)).astype(bf16)
    y = jnp.dot(gbuf[...], wd_ref[...], preferred_element_type=f32)
    out = o_ref[...] + mod_ref[5:6, :] * y
    if final:
        out = out * lax.rsqrt(jnp.mean(out * out, axis=-1, keepdims=True) + EPS) * fg_ref[...]
    o_ref[...] = out


def out_proj_ffn(h, p, na_x, ret_x, mod, g2, w_pool, pool_scale, w_out, w_up, conv_w, conv_b, w_down, final_g, final):
    L = h.shape[0]
    tm = _row_tile(L)

    def halo3(n, halo=FFN_HALO):
        hb = tm // halo
        return [
            pl.BlockSpec((halo, n), lambda i: (jnp.maximum(i * hb - 1, 0), 0)),
            pl.BlockSpec((tm, n), lambda i: (i, 0)),
            pl.BlockSpec((halo, n), lambda i: (jnp.minimum((i + 1) * hb, L // halo - 1), 0)),
        ]

    const = lambda arr: pl.BlockSpec(arr.shape, lambda i: (0,) * arr.ndim, pipeline_mode=pl.Buffered(1))
    consts = (mod, g2, w_pool, pool_scale, w_out, w_up, conv_w, conv_b, w_down, final_g)
    return pl.pallas_call(
        functools.partial(_out_ffn_kernel, tm=tm, L=L, final=final),
        grid=(L // tm,),
        in_specs=(halo3(D_MODEL) + halo3(POOL_DIM, POOL_IN_BLOCK) + halo3(NA_DIM) + halo3(RET_DIM)
                  + [const(a) for a in consts]),
        out_specs=pl.BlockSpec((tm, D_MODEL), lambda i: (i, 0)),
        out_shape=jax.ShapeDtypeStruct((L, D_MODEL), f32),
        scratch_shapes=[pltpu.VMEM((tm + 2 * FFN_HALO, D_MODEL), bf16), pltpu.VMEM((tm, D_FF), bf16),
                        pltpu.VMEM((4, tm + 2 * (POOL_IN_HALO + POOL_MARGIN), POOL_DIM), f32)],
        compiler_params=_params("parallel"),
        name="out_proj_ffn",
    )(h, h, h, p, p, p, na_x, na_x, na_x, ret_x, ret_x, ret_x, *consts)


def kernel(x, c, ctx, c_ctx, w_mod, b_mod, norm1_g, w_in, pool_w, pool_scale, na_rpb, ret_decay_fwd, ret_decay_bwd,
           ret_gn_g, w_out, norm2_g, w_up, conv_w, conv_b, w_down, final_g):
    B, L, _ = x.shape
    assert B == 1
    n_ctx = ctx.shape[1]
    h = x[0]
    hc = ctx[0]
    mods = mod_vectors(c, c_ctx, w_mod, b_mod)
    rope = rope_tables(L // GRID_W)
    no_rope = rope_identity(n_ctx // GRID_W)
    zero_state = jnp.zeros((N_PAIRS, LANES, LANES), f32)
    fg = final_g.reshape(1, D_MODEL)
    w_down_t = w_down.reshape(DEPTH, D_MODEL, D_FF)
    stacks = (w_in, w_out, w_up, w_down_t)
    wi, wo, wu, wd_t = [stack[0].astype(bf16) for stack in stacks]
    bias_rows = na_bias_rows(na_rpb)
    for l in range(DEPTH):
        need_ctx = l < DEPTH - 1
        mx = mods[l, 0].reshape(6, D_MODEL)
        mc = mods[l, 1].reshape(6, D_MODEL)
        g1 = norm1_g[l].reshape(1, D_MODEL)
        g2 = norm2_g[l].reshape(1, D_MODEL)
        cb = conv_b[l].reshape(1, 2 * D_FF)
        wp = pool_blockdiag(pool_w[l])
        ps = pool_scale[l].reshape(1, POOL_DIM)
        lgf = ret_decay_fwd[l].astype(f32)
        lgb = ret_decay_bwd[l].astype(f32)
        lgf_lane = jnp.repeat(lgf, HEAD_DIM).reshape(1, RET_DIM)
        lgb_lane = jnp.repeat(lgb, HEAD_DIM).reshape(1, RET_DIM)
        gn = ret_gn_g[l].reshape(1, RET_DIM)
        ffn_w = (g2, wp, ps, wo, wu, conv_w[l], cb, wd_t.reshape(D_FF, D_MODEL), fg)

        pc, nac, retc, gatec = in_proj(hc, g1, mc, wi, no_rope)
        cast = [(stack, l + 1) for stack in stacks] if need_ctx else []
        px, nax, retx, gatex, *next_w = in_proj(h, g1, mx, wi, rope, cast)

        sfc, sbc, sf0, sb0 = retention_states(retc, lgf_lane, lgb_lane, zero_state, zero_state)
        sfx, sbx, _, _ = retention_states(retx, lgf_lane, lgb_lane, sf0, sb0)

        na_x = neighborhood_attention(nax, nac, bias_rows[l])
        ret_x = retention_outputs(retx, gatex, sfx, sbx, lgf, lgb, lgf_lane, lgb_lane, gn)
        h = out_proj_ffn(h, px, na_x, ret_x, mx, *ffn_w, final=not need_ctx)

        if need_ctx:
            na_c = context_attention(nac)
            ret_c = retention_outputs(retc, gatec, sfc, sbc, lgf, lgb, lgf_lane, lgb_lane, gn)
            hc = out_proj_ffn(hc, pc, na_c, ret_c, mc, *ffn_w, final=False)
            wi, wo, wu, wd_t = next_w
    return h[None]
```

```python
import functools

import numpy as np
import jax
import jax.numpy as jnp
from jax import lax
from jax.experimental import pallas as pl
from jax.experimental.pallas import tpu as pltpu

f32 = jnp.float32
bf16 = jnp.bfloat16

D_MODEL = 1024
DEPTH = 4
GRID_W = 64
HEAD_DIM = 64
POOL_GROUPS = 4
POOL_WINDOWS = (2, 4, 8, 16)
POOL_DIM = 256
NA_HEADS = 6
NA_DIM = 384
NA_KH = 8
NA_KW = 16
RET_HEADS = 6
RET_DIM = 384
RET_CHUNK = 128
D_FF = 2816
ROPE_BASE = 10000.0
ROT_QUARTER = HEAD_DIM // 4
EPS = 1e-6
NEG_INF = -1e30
IN_DIM = POOL_DIM + 3 * NA_DIM + 4 * RET_DIM
OFF_NA = POOL_DIM
OFF_RET = OFF_NA + 3 * NA_DIM
OFF_GATE = OFF_RET + 3 * RET_DIM

LANES = 128
N_PAIRS = NA_HEADS // 2
VMEM_LIMIT = 56 * 1024 * 1024

LOG2E = float(np.log2(np.e))
NT_DIMS = (((1,), (1,)), ((), ()))
TN_DIMS = (((0,), (0,)), ((), ()))


def _params(*sem):
    return pltpu.CompilerParams(dimension_semantics=sem, vmem_limit_bytes=VMEM_LIMIT)


class _Entry:
    def __init__(self, stack, *index):
        self.stack = stack
        self.index = index
        self.shape = stack.shape[len(index):]


def _operand(a):
    return a.stack if isinstance(a, _Entry) else a


def _whole(a, **kw):
    if isinstance(a, _Entry):
        tail = (0,) * len(a.shape)
        return pl.BlockSpec((None,) * len(a.index) + a.shape, lambda *_: a.index + tail, **kw)
    return pl.BlockSpec(a.shape, lambda *_: (0,) * a.ndim, **kw)


def _row_tile(L):
    return min(L, 512)


def _silu(x):
    return x * jax.nn.sigmoid(x)


def _lo_half():
    return lax.broadcasted_iota(jnp.int32, (1, LANES), 1) < HEAD_DIM


def _mod_kernel(s_ref, w_ref, b_ref, o_ref):
    a = _silu(s_ref[...]).astype(bf16)
    o_ref[...] = jnp.dot(a, w_ref[...].astype(bf16), preferred_element_type=f32) + b_ref[...]


MOD_BLOCK = 3 * D_MODEL


def mod_vectors(c, c_ctx, w_mod, b_mod):
    s = jnp.zeros((8, D_MODEL), f32).at[0].set(c[0]).at[1].set(c_ctx)
    return pl.pallas_call(
        _mod_kernel,
        grid=(DEPTH, 6 * D_MODEL // MOD_BLOCK),
        in_specs=[
            pl.BlockSpec((8, D_MODEL), lambda l, j: (0, 0)),
            pl.BlockSpec((None, D_MODEL, MOD_BLOCK), lambda l, j: (l, 0, j)),
            pl.BlockSpec((None, 1, MOD_BLOCK), lambda l, j: (l, 0, j)),
        ],
        out_specs=pl.BlockSpec((None, 8, MOD_BLOCK), lambda l, j: (l, 0, j)),
        out_shape=jax.ShapeDtypeStruct((DEPTH, 8, 6 * D_MODEL), f32),
        compiler_params=_params("parallel", "parallel"),
        name="mod_vectors",
    )(s, w_mod, b_mod.reshape(DEPTH, 1, 6 * D_MODEL))


def _in_proj_kernel(h_ref, g_ref, mod_ref, w_ref, cos_row_ref, sin_row_ref, cos_col_ref, sin_col_ref, *refs):
    n_cast = (len(refs) - 4) // 2
    cast_in = refs[:n_cast]
    p_ref, na_ref, ret_ref, gate_ref = refs[n_cast:n_cast + 4]
    for src, dst in zip(cast_in, refs[n_cast + 4:]):
        dst[...] = src[...].astype(bf16)
    h = h_ref[...]
    y = h * lax.rsqrt(jnp.mean(h * h, axis=-1, keepdims=True) + EPS) * g_ref[...]
    a = (y * (1.0 + mod_ref[1:2, :]) + mod_ref[0:1, :]).astype(bf16)

    def proj(lo, n):
        return jnp.dot(a, w_ref[:, lo:lo + n], preferred_element_type=f32)

    p_ref[...] = proj(0, POOL_DIM)
    na = proj(OFF_NA, 3 * NA_DIM)
    na_ref[:, 0:NA_DIM] = (na[:, 0:NA_DIM] * (HEAD_DIM ** -0.5 * LOG2E)).astype(bf16)
    na_ref[:, NA_DIM:] = na[:, NA_DIM:].astype(bf16)
    lane = lax.broadcasted_iota(jnp.int32, (1, LANES), 1)
    first = lane % (2 * ROT_QUARTER) < ROT_QUARTER
    row_lane = lane % HEAD_DIM < HEAD_DIM // 2

    def table(row_ref, col_ref):
        col = col_ref[...]
        return jnp.concatenate(
            [jnp.where(row_lane, jnp.broadcast_to(row_ref[j:j + 1, :], col.shape), col)
             for j in range(row_ref.shape[0])], axis=0)

    cos = table(cos_row_ref, cos_col_ref)
    sin = table(sin_row_ref, sin_col_ref)
    qk = proj(OFF_RET, 2 * RET_DIM)
    for slab in range(2 * RET_DIM // LANES):
        lo = LANES * slab
        x = qk[:, lo:lo + LANES]
        partner = jnp.where(first, pltpu.roll(x, LANES - ROT_QUARTER, 1), pltpu.roll(x, ROT_QUARTER, 1))
        x = x * cos + partner * sin
        if lo >= RET_DIM:
            x = x * HEAD_DIM ** -0.5
        ret_ref[:, lo:lo + LANES] = x.astype(bf16)
    vg = proj(OFF_RET + 2 * RET_DIM, 2 * RET_DIM)
    ret_ref[:, 2 * RET_DIM:] = vg[:, :RET_DIM].astype(bf16)
    gate_ref[...] = vg[:, RET_DIM:]


def in_proj(h, g, mod, w_in, rope, cast=()):
    L = h.shape[0]
    tm = _row_tile(L)
    steps = L // tm
    assert tm % GRID_W == 0
    cast_in, cast_out, cast_shapes = [], [], []
    for stack, layer in cast:
        _, k, n = stack.shape
        nblk = max(b for b in range(1, steps + 1) if steps % b == 0 and k % (16 * b) == 0)
        block = functools.partial(lambda i, nblk: i * nblk // steps, nblk=nblk)
        cast_in.append(pl.BlockSpec((None, k // nblk, n),
                                    functools.partial(lambda i, layer, block: (layer, block(i), 0), layer=layer, block=block)))
        cast_out.append(pl.BlockSpec((k // nblk, n), functools.partial(lambda i, block: (block(i), 0), block=block)))
        cast_shapes.append(jax.ShapeDtypeStruct((k, n), bf16))
    row = lambda n: pl.BlockSpec((tm, n), lambda i: (i, 0))
    grid_rows = pl.BlockSpec((tm // GRID_W, LANES), lambda i: (i, 0))
    grid_cols = pl.BlockSpec((GRID_W, LANES), lambda i: (0, 0))
    return pl.pallas_call(
        _in_proj_kernel,
        grid=(L // tm,),
        in_specs=[row(D_MODEL), _whole(g), _whole(mod), _whole(w_in),
                  grid_rows, grid_rows, grid_cols, grid_cols] + cast_in,
        out_specs=[row(POOL_DIM), row(3 * NA_DIM), row(3 * RET_DIM), row(RET_DIM)] + cast_out,
        out_shape=[
            jax.ShapeDtypeStruct((L, POOL_DIM), f32),
            jax.ShapeDtypeStruct((L, 3 * NA_DIM), bf16),
            jax.ShapeDtypeStruct((L, 3 * RET_DIM), bf16),
            jax.ShapeDtypeStruct((L, RET_DIM), f32),
        ] + cast_shapes,
        compiler_params=_params("arbitrary"),
        name="in_proj",
    )(h, _operand(g), _operand(mod), w_in, *rope, *[stack for stack, _ in cast])


def rope_tables(rows):
    inv = ROPE_BASE ** (-jnp.arange(ROT_QUARTER, dtype=f32) / ROT_QUARTER)

    def factors(n):
        ang = jnp.arange(n, dtype=f32)[:, None] * inv[None, :]
        cos = jnp.tile(jnp.cos(ang), (1, LANES // ROT_QUARTER))
        sin = jnp.tile(jnp.concatenate([-jnp.sin(ang), jnp.sin(ang)], axis=-1), (1, LANES // (2 * ROT_QUARTER)))
        return cos, sin

    return factors(rows) + factors(GRID_W)


def rope_identity(rows):
    ones = lambda n: jnp.ones((n, LANES), f32)
    zeros = lambda n: jnp.zeros((n, LANES), f32)
    return ones(rows), zeros(rows), ones(GRID_W), zeros(GRID_W)


POOL_HALO = 8
POOL_MARGIN = 8


def _pooled_rows(prev, x_ref, nxt, w_ref, scale_ref, lvl_ref, *, first, last, row0, L):
    assert POOL_WINDOWS == (2, 4, 8, 16)
    tm = x_ref.shape[0]
    halo = prev.shape[0]
    data = tm + 2 * halo
    m0 = POOL_MARGIN
    zeros = jnp.zeros((m0, POOL_DIM), f32)
    for k in range(3):
        lvl_ref[k, 0:m0, :] = zeros
        lvl_ref[k, m0 + data:, :] = zeros
    lvl_ref[0, m0:m0 + halo, :] = jnp.where(first, 0.0, prev)
    lvl_ref[0, m0 + halo:m0 + halo + tm, :] = x_ref[...]
    lvl_ref[0, m0 + halo + tm:m0 + data, :] = jnp.where(last, 0.0, nxt)

    def shifted(k, d, rows=data, start=m0):
        return lvl_ref[k, start + d:start + d + rows, :]

    lvl_ref[1, m0:m0 + data, :] = shifted(0, -1) + shifted(0, 0)
    lvl_ref[2, m0:m0 + data, :] = shifted(1, -1) + shifted(1, 1)
    lvl_ref[3, m0:m0 + data, :] = shifted(2, -2) + shifted(2, 2)
    out_rows = data - 2 * POOL_HALO
    centre = m0 + POOL_HALO
    x = shifted(0, 0, out_rows, centre)
    sums = [shifted(k, 0, out_rows, centre) for k in (1, 2, 3)]
    sums.append(shifted(3, -4, out_rows, centre) + shifted(3, 4, out_rows, centre))
    lane = lax.broadcasted_iota(jnp.int32, (1, POOL_DIM), 1)
    gdim = POOL_DIM // POOL_GROUPS
    total = sums[-1]
    half = jnp.full((1, POOL_DIM), POOL_WINDOWS[-1] // 2, jnp.int32)
    for gi in range(POOL_GROUPS - 2, -1, -1):
        sel = lane < gdim * (gi + 1)
        total = jnp.where(sel, sums[gi], total)
        half = jnp.where(sel, POOL_WINDOWS[gi] // 2, half)
    t = row0 - (halo - POOL_HALO) + lax.broadcasted_iota(jnp.int32, (out_rows, 1), 0)
    count = jnp.maximum(jnp.minimum(t + half, L) - jnp.maximum(t - half, 0), 1)
    m = (total / count.astype(f32) - x).astype(bf16)
    return (jnp.dot(m, w_ref[...], preferred_element_type=f32) * scale_ref[...]).astype(bf16)


def pool_blockdiag(pool_w):
    eye = jnp.eye(POOL_GROUPS, dtype=pool_w.dtype)
    w = jnp.einsum('...gcd,gh->...gchd', pool_w, eye)
    return w.reshape(pool_w.shape[:-3] + (POOL_DIM, POOL_DIM)).astype(bf16)


NA_RB = 4
NA_QT = NA_RB * GRID_W
NA_WIN = NA_RB + NA_KH
NA_KB = NA_WIN // NA_RB
NA_DR_MASKED = 2 * NA_KH - 1
NA_VARIANTS = 3


def na_bias_rows(rpb):
    assert LANES == 2 * GRID_W
    w = jnp.zeros(rpb.shape[:-1] + (LANES,), f32)
    w = w.at[..., :NA_KW].set(rpb[..., NA_KW - 1:])
    w = w.at[..., LANES - (NA_KW - 1):].set(rpb[..., :NA_KW - 1])
    return w * LOG2E


def _na_fill_bias(rows_ref, bias_scr, dr_maps):
    lane = lax.broadcasted_iota(jnp.int32, (GRID_W, LANES), 1)
    qcol = lax.broadcasted_iota(jnp.int32, (GRID_W, LANES), 0)
    kcol = lane % GRID_W
    ws = jnp.clip(qcol - NA_KW // 2, 0, GRID_W - NA_KW)
    in_window = jnp.logical_and(kcol >= ws, kcol < ws + NA_KW)
    masked = jnp.full((GRID_W, LANES), NEG_INF, f32)
    for h in range(NA_HEADS):
        for dr in range(NA_DR_MASKED + 1):
            if dr == NA_DR_MASKED:
                tile = masked
            else:
                row = jnp.broadcast_to(rows_ref[h, dr:dr + 1, :], (GRID_W, LANES))
                t = pltpu.roll(row, 0, 1, stride=1, stride_axis=0)
                t = jnp.where(lane < GRID_W, t, pltpu.roll(t, GRID_W, 1))
                tile = jnp.where(in_window, t, NEG_INF)
            for var in range(NA_VARIANTS):
                for j, i in zip(*np.nonzero(dr_maps[var] == dr)):
                    half = GRID_W * (int(i) % 2)
                    bias_scr[var, h, GRID_W * j:GRID_W * (j + 1), GRID_W * i:GRID_W * (i + 1)] = (
                        tile[:, half:half + GRID_W])


def _na_key_base(g, nb):
    return jnp.clip(g - 1, 0, nb - NA_KB)


def _na_dr_maps(rows):
    nb = rows // NA_RB
    maps = []
    for g in range(nb):
        kb = NA_RB * min(max(g - 1, 0), nb - NA_KB)
        m = np.full((NA_RB, NA_WIN), NA_DR_MASKED)
        for j in range(NA_RB):
            r = NA_RB * g + j
            rs = min(max(r - NA_KH // 2, 0), rows - NA_KH)
            for i in range(rs - kb, rs - kb + NA_KH):
                m[j, i] = kb + i - r + NA_KH - 1
        maps.append(m)
    variants = [maps[0], maps[1], maps[nb - 1]]
    for g in range(nb):
        assert (maps[g] == variants[0 if g == 0 else 2 if g == nb - 1 else 1]).all()
    return variants


def _per_head_rows(x, lo_half):
    zero = jnp.zeros_like(x)
    return jnp.concatenate([jnp.where(lo_half, x, zero), jnp.where(lo_half, zero, x)], axis=0)


def _with_ones(v, lo_half, sub):
    ones = jnp.ones_like(v)
    return jnp.where(lo_half, v, ones) if sub == 0 else jnp.where(lo_half, ones, v)


def _softmax_pv(s_parts, v_parts):
    maxes = [jnp.max(s, axis=-1, keepdims=True) for s in s_parts]
    m = functools.reduce(jnp.maximum, maxes)
    out = None
    for s, v, mk in zip(s_parts, v_parts, maxes):
        pv = jnp.dot(jnp.exp2(s - mk).astype(bf16), v, preferred_element_type=f32) * jnp.exp2(mk - m)
        out = pv if out is None else out + pv
    return out / pltpu.roll(out, HEAD_DIM, 1)


def _na_kernel(q_ref, k0_ref, k1_ref, k2_ref, v0_ref, v1_ref, v2_ref, kc_ref, vc_ref, rows_ref, o_ref,
               bias_scr, *, nb, dr_maps):
    g = pl.program_id(0)

    @pl.when(g == 0)
    def _():
        _na_fill_bias(rows_ref, bias_scr, dr_maps)

    var = jnp.where(g == 0, 0, jnp.where(g == nb - 1, 2, 1))
    lo_half = _lo_half()
    k_refs = (k0_ref, k1_ref, k2_ref)
    v_refs = (v0_ref, v1_ref, v2_ref)
    for hp in range(N_PAIRS):
        sl = slice(LANES * hp, LANES * (hp + 1))
        q2 = q_ref[:, sl]
        values = [vr[:, sl] for vr in v_refs] + [vc_ref[:, sl]]
        outs = []
        for sub in range(2):
            values_sub = [_with_ones(v, lo_half, sub) for v in values]
            qm = jnp.where(lo_half if sub == 0 else jnp.logical_not(lo_half), q2, jnp.zeros_like(q2))
            scores = []
            for kb in range(NA_KB):
                s = lax.dot_general(qm, k_refs[kb][:, sl], NT_DIMS, preferred_element_type=f32)
                tb = bias_scr[var, 2 * hp + sub, :, NA_QT * kb:NA_QT * (kb + 1)]
                scores.append(jnp.where(tb < 0.5 * NEG_INF, NEG_INF, s + tb))
            scores.append(lax.dot_general(qm, kc_ref[:, sl], NT_DIMS, preferred_element_type=f32))
            outs.append(_softmax_pv(scores, values_sub))
        o_ref[:, sl] = jnp.where(lo_half, outs[0], outs[1]).astype(bf16)


def neighborhood_attention(na, na_ctx, bias_rows):
    L = na.shape[0]
    rows = L // GRID_W
    nb = rows // NA_RB
    assert rows % NA_RB == 0 and nb > NA_KB
    ctx = na_ctx.shape[0]
    blk = lambda off, col: pl.BlockSpec((NA_QT, NA_DIM), lambda g: (_na_key_base(g, nb) + off, col))
    return pl.pallas_call(
        functools.partial(_na_kernel, nb=nb, dr_maps=_na_dr_maps(rows)),
        grid=(nb,),
        in_specs=[
            pl.BlockSpec((NA_QT, NA_DIM), lambda g: (g, 0)),
            blk(0, 1), blk(1, 1), blk(2, 1),
            blk(0, 2), blk(1, 2), blk(2, 2),
            pl.BlockSpec((ctx, NA_DIM), lambda g: (0, 1)),
            pl.BlockSpec((ctx, NA_DIM), lambda g: (0, 2)),
            _whole(bias_rows),
        ],
        out_specs=pl.BlockSpec((NA_QT, NA_DIM), lambda g: (g, 0)),
        out_shape=jax.ShapeDtypeStruct((L, NA_DIM), bf16),
        scratch_shapes=[pltpu.VMEM((NA_VARIANTS, NA_HEADS, NA_QT, NA_WIN * GRID_W), f32)],
        compiler_params=_params("arbitrary"),
        name="neighborhood_attention",
    )(na, na, na, na, na, na, na, na_ctx, na_ctx, _operand(bias_rows))


def _ctx_attn_kernel(q_ref, k_ref, v_ref, o_ref):
    lo_half = _lo_half()
    for hp in range(N_PAIRS):
        sl = slice(LANES * hp, LANES * (hp + 1))
        q2 = q_ref[:, sl]
        k2 = k_ref[:, sl]
        v2 = v_ref[:, sl]
        outs = []
        for sub in range(2):
            qm = jnp.where(lo_half if sub == 0 else jnp.logical_not(lo_half), q2, jnp.zeros_like(q2))
            s = lax.dot_general(qm, k2, NT_DIMS, preferred_element_type=f32)
            outs.append(_softmax_pv([s], [_with_ones(v2, lo_half, sub)]))
        o_ref[:, sl] = jnp.where(lo_half, outs[0], outs[1]).astype(bf16)


def context_attention(na_ctx):
    ctx = na_ctx.shape[0]
    col = lambda c: pl.BlockSpec((ctx, NA_DIM), lambda i: (0, c))
    return pl.pallas_call(
        _ctx_attn_kernel,
        grid=(1,),
        in_specs=[col(0), col(1), col(2)],
        out_specs=pl.BlockSpec((ctx, NA_DIM), lambda i: (0, 0)),
        out_shape=jax.ShapeDtypeStruct((ctx, NA_DIM), bf16),
        compiler_params=_params("arbitrary"),
        name="context_attention",
    )(na_ctx, na_ctx, na_ctx)


RET_STEP = 8


def _pair_blockmask():
    r = lax.broadcasted_iota(jnp.int32, (LANES, LANES), 0) < HEAD_DIM
    c = lax.broadcasted_iota(jnp.int32, (LANES, LANES), 1) < HEAD_DIM
    return r == c


def _ret_state_kernel(k_ref, v_ref, lgf_ref, lgb_ref, s0f_ref, s0b_ref, sf_out, sb_out, sf_fin, sb_fin, *, n_chunks):
    c = lax.broadcasted_iota(jnp.int32, (RET_CHUNK, 1), 0).astype(f32)
    blockmask = _pair_blockmask()
    lgf = lgf_ref[...]
    lgb = lgb_ref[...]
    kdec_f = jnp.exp((RET_CHUNK - 1.0 - c) * lgf)
    kdec_b = jnp.exp(c * lgb)
    cdec_f = jnp.exp(float(RET_CHUNK) * lgf)
    cdec_b = jnp.exp(float(RET_CHUNK) * lgb)

    def advance(state, chunk, kdec, cdec, out):
        rows = pl.ds(pl.multiple_of(chunk * RET_CHUNK, RET_CHUNK), RET_CHUNK)
        out[chunk] = state.astype(bf16)
        kd = (k_ref[rows, :].astype(f32) * kdec).astype(bf16)
        kv = lax.dot_general(kd, v_ref[rows, :], TN_DIMS, preferred_element_type=f32)
        return state * cdec + jnp.where(blockmask, kv, 0.0)

    def body(n, carry):
        sf, sb = carry
        return (advance(sf, n, kdec_f, cdec_f, sf_out), advance(sb, n_chunks - 1 - n, kdec_b, cdec_b, sb_out))

    sf, sb = lax.fori_loop(0, n_chunks, body, (s0f_ref[...], s0b_ref[...]), unroll=min(n_chunks, 8))
    sf_fin[...] = sf
    sb_fin[...] = sb


def retention_states(ret, lgf_lane, lgb_lane, s0f, s0b):
    L = ret.shape[0]
    N = L // RET_CHUNK
    col = lambda c: pl.BlockSpec((L, LANES), lambda hp: (0, c * N_PAIRS + hp))

    def vec(a):
        if isinstance(a, _Entry):
            return pl.BlockSpec((None,) * len(a.index) + (1, LANES), lambda hp: a.index + (0, hp))
        return pl.BlockSpec((1, LANES), lambda hp: (0, hp))

    st = pl.BlockSpec((None, LANES, LANES), lambda hp: (hp, 0, 0))
    per_chunk = pl.BlockSpec((N, None, LANES, LANES), lambda hp: (0, hp, 0, 0))
    st_shape = jax.ShapeDtypeStruct((N_PAIRS, LANES, LANES), f32)
    all_shape = jax.ShapeDtypeStruct((N, N_PAIRS, LANES, LANES), bf16)
    return pl.pallas_call(
        functools.partial(_ret_state_kernel, n_chunks=N),
        grid=(N_PAIRS,),
        in_specs=[col(1), col(2), vec(lgf_lane), vec(lgb_lane), st, st],
        out_specs=[per_chunk, per_chunk, st, st],
        out_shape=[all_shape, all_shape, st_shape, st_shape],
        compiler_params=_params("parallel"),
        name="retention_states",
    )(ret, ret, _operand(lgf_lane), _operand(lgb_lane), s0f, s0b)


def _ret_out_kernel(lgf_s, lgb_s, q_ref, k_ref, v_ref, gate_ref, sf_ref, sb_ref, lgf_ref, lgb_ref, gn_ref, o_ref,
                    *, cs, layer):
    lo_half = _lo_half()
    diff = (lax.broadcasted_iota(jnp.int32, (RET_CHUNK, RET_CHUNK), 0)
            - lax.broadcasted_iota(jnp.int32, (RET_CHUNK, RET_CHUNK), 1)).astype(f32)
    c = lax.broadcasted_iota(jnp.int32, (RET_CHUNK, 1), 0).astype(f32)
    qdec_f = jnp.exp((c + 1.0) * lgf_ref[...])
    qdec_b = jnp.exp((RET_CHUNK - c) * lgb_ref[...])
    decay = [jnp.where(diff >= 0, jnp.exp(diff * lgf_s[layer, h]), 0.0)
             + jnp.where(diff <= 0, jnp.exp(-diff * lgb_s[layer, h]), 0.0) for h in range(RET_HEADS)]
    inv_hd = 1.0 / HEAD_DIM

    def head_mean(t):
        m0 = jnp.sum(jnp.where(lo_half, t, 0.0), axis=-1, keepdims=True) * inv_hd
        m1 = jnp.sum(jnp.where(lo_half, 0.0, t), axis=-1, keepdims=True) * inv_hd
        return jnp.where(lo_half, m0, m1)

    for ci in range(cs):
        rows = slice(RET_CHUNK * ci, RET_CHUNK * (ci + 1))
        q = q_ref[rows, :]
        qf32 = q.astype(f32)
        q_f = (qf32 * qdec_f).astype(bf16)
        q_b = (qf32 * qdec_b).astype(bf16)
        for hp in range(N_PAIRS):
            sl = slice(LANES * hp, LANES * (hp + 1))
            q2 = q[:, sl]
            k2 = k_ref[rows, sl]
            v2 = v_ref[rows, sl]
            a = lax.dot_general(q2, _per_head_rows(k2, lo_half), NT_DIMS, preferred_element_type=f32)
            lhs = [(a[:, :RET_CHUNK] * decay[2 * hp]).astype(bf16), (a[:, RET_CHUNK:] * decay[2 * hp + 1]).astype(bf16),
                   q_f[:, sl], q_b[:, sl]]
            rhs = [_per_head_rows(v2, lo_half), sf_ref[ci, hp], sb_ref[ci, hp]]
            y = jnp.dot(jnp.concatenate(lhs, axis=1), jnp.concatenate(rhs, axis=0), preferred_element_type=f32)
            d = y - head_mean(y)
            yn = d * lax.rsqrt(head_mean(d * d) + EPS) * gn_ref[:, sl]
            o_ref[rows, sl] = (_silu(gate_ref[rows, sl]) * yn).astype(bf16)


def retention_outputs(ret, gate, sf, sb, lgf, lgb, layer, lgf_lane, lgb_lane, gn):
    L = ret.shape[0]
    N = L // RET_CHUNK
    cs = min(RET_STEP, N)
    col = lambda c: pl.BlockSpec((cs * RET_CHUNK, RET_DIM), lambda n, *_: (n, c))
    st = pl.BlockSpec((cs, N_PAIRS, LANES, LANES), lambda n, *_: (n, 0, 0, 0))
    return pl.pallas_call(
        functools.partial(_ret_out_kernel, cs=cs, layer=layer),
        grid_spec=pltpu.PrefetchScalarGridSpec(
            num_scalar_prefetch=2,
            grid=(N // cs,),
            in_specs=[col(0), col(1), col(2), col(0), st, st, _whole(lgf_lane), _whole(lgb_lane), _whole(gn)],
            out_specs=pl.BlockSpec((cs * RET_CHUNK, RET_DIM), lambda n, *_: (n, 0)),
        ),
        out_shape=jax.ShapeDtypeStruct((L, RET_DIM), bf16),
        compiler_params=_params("parallel"),
        name="retention_outputs",
    )(lgf, lgb, ret, ret, ret, gate, sf, sb, _operand(lgf_lane), _operand(lgb_lane), _operand(gn))


FFN_HALO = 16
FFN_TF = 256
POOL_IN_HALO = FFN_HALO + POOL_HALO
POOL_IN_BLOCK = 2 * FFN_HALO


def _out_ffn_kernel(hp_ref, h_ref, hn_ref, pp_ref, p_ref, pn_ref, np_ref, n_ref, nn_ref, rp_ref, r_ref, rn_ref,
                    mod_ref, g_ref, wp_ref, ps_ref, wo_ref, wu_ref, cw_ref, cb_ref, wd_ref, fg_ref, o_ref,
                    aext, gbuf, lvl, *, tm, L, final):
    i = pl.program_id(0)
    ni = pl.num_programs(0)
    ext = tm + 2 * FFN_HALO

    def rows3(prev, cur, nxt):
        return jnp.concatenate([prev[...], cur[...], nxt[...]], axis=0)

    y = jnp.dot(jnp.concatenate([rows3(np_ref, n_ref, nn_ref), rows3(rp_ref, r_ref, rn_ref)], axis=1),
                wo_ref[POOL_DIM:, :], preferred_element_type=f32)
    pool_x = _pooled_rows(pp_ref[POOL_IN_BLOCK - POOL_IN_HALO:, :], p_ref, pn_ref[:POOL_IN_HALO, :], wp_ref, ps_ref,
                          lvl, first=i == 0, last=i == ni - 1, row0=i * tm, L=L)
    y = y + jnp.dot(pool_x, wo_ref[:POOL_DIM, :], preferred_element_type=f32)
    hx = rows3(hp_ref, h_ref, hn_ref) + mod_ref[2:3, :] * y
    o_ref[...] = hx[FFN_HALO:FFN_HALO + tm]
    a = hx * lax.rsqrt(jnp.mean(hx * hx, axis=-1, keepdims=True) + EPS) * g_ref[...]
    a = a * (1.0 + mod_ref[4:5, :]) + mod_ref[3:4, :]
    row = lax.broadcasted_iota(jnp.int32, (ext, 1), 0)
    inside = jnp.logical_and(jnp.logical_or(row >= FFN_HALO, i > 0), jnp.logical_or(row < FFN_HALO + tm, i < ni - 1))
    aext[...] = jnp.where(inside, a, 0.0).astype(bf16)
    ae = aext[...]

    def branch(lo):
        u = jnp.dot(ae, wu_ref[:, lo:lo + FFN_TF], preferred_element_type=f32)
        up = pltpu.roll(u, 1, 0)[FFN_HALO:FFN_HALO + tm]
        un = pltpu.roll(u, ext - 1, 0)[FFN_HALO:FFN_HALO + tm]
        uc = u[FFN_HALO:FFN_HALO + tm]
        cw = cw_ref[:, lo:lo + FFN_TF]
        return up * cw[0:1, :] + uc * cw[1:2, :] + un * cw[2:3, :] + cb_ref[:, lo:lo + FFN_TF]

    for s in range(D_FF // FFN_TF):
        lo = FFN_TF * s
        gbuf[:, lo:lo + FFN_TF] = (_silu(branch(lo)) * branch(D_FF + lo)).astype(bf16)
    y = jnp.dot(gbuf[...], wd_ref[...], preferred_element_type=f32)
    out = o_ref[...] + mod_ref[5:6, :] * y
    if final:
        out = out * lax.rsqrt(jnp.mean(out * out, axis=-1, keepdims=True) + EPS) * fg_ref[...]
    o_ref[...] = out


def out_proj_ffn(h, p, na_x, ret_x, mod, g2, w_pool, pool_scale, w_out, w_up, conv_w, conv_b, w_down, final_g, final):
    L = h.shape[0]
    tm = _row_tile(L)

    def halo3(n, halo=FFN_HALO):
        hb = tm // halo
        return [
            pl.BlockSpec((halo, n), lambda i: (jnp.maximum(i * hb - 1, 0), 0)),
            pl.BlockSpec((tm, n), lambda i: (i, 0)),
            pl.BlockSpec((halo, n), lambda i: (jnp.minimum((i + 1) * hb, L // halo - 1), 0)),
        ]

    consts = (mod, g2, w_pool, pool_scale, w_out, w_up, conv_w, conv_b, w_down, final_g)
    return pl.pallas_call(
        functools.partial(_out_ffn_kernel, tm=tm, L=L, final=final),
        grid=(L // tm,),
        in_specs=(halo3(D_MODEL) + halo3(POOL_DIM, POOL_IN_BLOCK) + halo3(NA_DIM) + halo3(RET_DIM)
                  + [_whole(a, pipeline_mode=pl.Buffered(1)) for a in consts]),
        out_specs=pl.BlockSpec((tm, D_MODEL), lambda i: (i, 0)),
        out_shape=jax.ShapeDtypeStruct((L, D_MODEL), f32),
        scratch_shapes=[pltpu.VMEM((tm + 2 * FFN_HALO, D_MODEL), bf16), pltpu.VMEM((tm, D_FF), bf16),
                        pltpu.VMEM((4, tm + 2 * (POOL_IN_HALO + POOL_MARGIN), POOL_DIM), f32)],
        compiler_params=_params("parallel"),
        name="out_proj_ffn",
    )(h, h, h, p, p, p, na_x, na_x, na_x, ret_x, ret_x, ret_x, *[_operand(a) for a in consts])


def kernel(x, c, ctx, c_ctx, w_mod, b_mod, norm1_g, w_in, pool_w, pool_scale, na_rpb, ret_decay_fwd, ret_decay_bwd,
           ret_gn_g, w_out, norm2_g, w_up, conv_w, conv_b, w_down, final_g):
    B, L, _ = x.shape
    assert B == 1
    n_ctx = ctx.shape[1]
    h = x[0]
    hc = ctx[0]
    mods = mod_vectors(c, c_ctx, w_mod, b_mod)
    rope = rope_tables(L // GRID_W)
    no_rope = rope_identity(n_ctx // GRID_W)
    zero_state = jnp.zeros((N_PAIRS, LANES, LANES), f32)
    fg = final_g.reshape(1, D_MODEL)
    stacks = (w_in, w_out, w_up, w_down)
    wi, wo, wu, wd = [stack[0].astype(bf16) for stack in stacks]
    mods = mods.reshape(DEPTH, 8, 6, D_MODEL)
    g1 = norm1_g.reshape(DEPTH, 1, D_MODEL)
    g2 = norm2_g.reshape(DEPTH, 1, D_MODEL)
    cb = conv_b.reshape(DEPTH, 1, 2 * D_FF)
    wp = pool_blockdiag(pool_w)
    ps = pool_scale.reshape(DEPTH, 1, POOL_DIM)
    bias_rows = na_bias_rows(na_rpb)
    lgf = ret_decay_fwd.astype(f32)
    lgb = ret_decay_bwd.astype(f32)
    lgf_lane = jnp.repeat(lgf, HEAD_DIM, axis=1).reshape(DEPTH, 1, RET_DIM)
    lgb_lane = jnp.repeat(lgb, HEAD_DIM, axis=1).reshape(DEPTH, 1, RET_DIM)
    gn = ret_gn_g.reshape(DEPTH, 1, RET_DIM)
    for l in range(DEPTH):
        need_ctx = l < DEPTH - 1
        at = lambda stack: _Entry(stack, l)
        mx = _Entry(mods, l, 0)
        mc = _Entry(mods, l, 1)
        ret_p = (lgf, lgb, l, at(lgf_lane), at(lgb_lane), at(gn))
        ffn_w = (at(g2), at(wp), at(ps), wo, wu, at(conv_w), at(cb), wd, fg)

        pc, nac, retc, gatec = in_proj(hc, at(g1), mc, wi, no_rope)
        cast = [(stack, l + 1) for stack in stacks] if need_ctx else []
        px, nax, retx, gatex, *next_w = in_proj(h, at(g1), mx, wi, rope, cast)

        sfc, sbc, sf0, sb0 = retention_states(retc, at(lgf_lane), at(lgb_lane), zero_state, zero_state)
        sfx, sbx, _, _ = retention_states(retx, at(lgf_lane), at(lgb_lane), sf0, sb0)

        na_x = neighborhood_attention(nax, nac, at(bias_rows))
        ret_x = retention_outputs(retx, gatex, sfx, sbx, *ret_p)
        h = out_proj_ffn(h, px, na_x, ret_x, mx, *ffn_w, final=not need_ctx)

        if need_ctx:
            na_c = context_attention(nac)
            ret_c = retention_outputs(retc, gatec, sfc, sbc, *ret_p)
            hc = out_proj_ffn(hc, pc, na_c, ret_c, mc, *ffn_w, final=False)
            wi, wo, wu, wd = next_w
    return h[None]
```

```python
import functools

import numpy as np
import jax
import jax.numpy as jnp
from jax import lax
from jax.experimental import pallas as pl
from jax.experimental.pallas import tpu as pltpu

f32 = jnp.float32
bf16 = jnp.bfloat16

D_MODEL = 1024
DEPTH = 4
GRID_W = 64
HEAD_DIM = 64
POOL_GROUPS = 4
POOL_WINDOWS = (2, 4, 8, 16)
POOL_DIM = 256
NA_HEADS = 6
NA_DIM = 384
NA_KH = 8
NA_KW = 16
RET_HEADS = 6
RET_DIM = 384
RET_CHUNK = 128
D_FF = 2816
ROPE_BASE = 10000.0
ROT_QUARTER = HEAD_DIM // 4
EPS = 1e-6
NEG_INF = -1e30
IN_DIM = POOL_DIM + 3 * NA_DIM + 4 * RET_DIM
OFF_NA = POOL_DIM
OFF_RET = OFF_NA + 3 * NA_DIM
OFF_GATE = OFF_RET + 3 * RET_DIM

LANES = 128
N_PAIRS = NA_HEADS // 2
VMEM_LIMIT = 56 * 1024 * 1024

LOG2E = float(np.log2(np.e))
NT_DIMS = (((1,), (1,)), ((), ()))
TN_DIMS = (((0,), (0,)), ((), ()))


def _params(*sem):
    return pltpu.CompilerParams(dimension_semantics=sem, vmem_limit_bytes=VMEM_LIMIT)


def _row_tile(L):
    return min(L, 512)


def _silu(x):
    return x * jax.nn.sigmoid(x)


def _lo_half():
    return lax.broadcasted_iota(jnp.int32, (1, LANES), 1) < HEAD_DIM


def _mod_kernel(s_ref, w_ref, b_ref, o_ref):
    a = _silu(s_ref[...]).astype(bf16)
    o_ref[...] = jnp.dot(a, w_ref[...].astype(bf16), preferred_element_type=f32) + b_ref[...]


MOD_BLOCK = 3 * D_MODEL


def mod_vectors(c, c_ctx, w_mod, b_mod):
    s = jnp.zeros((8, D_MODEL), f32).at[0].set(c[0]).at[1].set(c_ctx)
    return pl.pallas_call(
        _mod_kernel,
        grid=(DEPTH, 6 * D_MODEL // MOD_BLOCK),
        in_specs=[
            pl.BlockSpec((8, D_MODEL), lambda l, j: (0, 0)),
            pl.BlockSpec((None, D_MODEL, MOD_BLOCK), lambda l, j: (l, 0, j)),
            pl.BlockSpec((None, 1, MOD_BLOCK), lambda l, j: (l, 0, j)),
        ],
        out_specs=pl.BlockSpec((None, 8, MOD_BLOCK), lambda l, j: (l, 0, j)),
        out_shape=jax.ShapeDtypeStruct((DEPTH, 8, 6 * D_MODEL), f32),
        compiler_params=_params("parallel", "parallel"),
        name="mod_vectors",
    )(s, w_mod, b_mod.reshape(DEPTH, 1, 6 * D_MODEL))


def _in_proj_kernel(h_ref, g_ref, mod_ref, w_ref, cos_row_ref, sin_row_ref, cos_col_ref, sin_col_ref, *refs):
    n_cast = (len(refs) - 4) // 2
    cast_in = refs[:n_cast]
    p_ref, na_ref, ret_ref, gate_ref = refs[n_cast:n_cast + 4]
    for src, dst in zip(cast_in, refs[n_cast + 4:]):
        dst[...] = src[...].astype(bf16)
    h = h_ref[...]
    y = h * lax.rsqrt(jnp.mean(h * h, axis=-1, keepdims=True) + EPS) * g_ref[...]
    a = (y * (1.0 + mod_ref[1:2, :]) + mod_ref[0:1, :]).astype(bf16)

    def proj(lo, n):
        return jnp.dot(a, w_ref[:, lo:lo + n], preferred_element_type=f32)

    p_ref[...] = proj(0, POOL_DIM)
    na = proj(OFF_NA, 3 * NA_DIM)
    na_ref[:, 0:NA_DIM] = (na[:, 0:NA_DIM] * (HEAD_DIM ** -0.5 * LOG2E)).astype(bf16)
    na_ref[:, NA_DIM:] = na[:, NA_DIM:].astype(bf16)
    lane = lax.broadcasted_iota(jnp.int32, (1, LANES), 1)
    first = lane % (2 * ROT_QUARTER) < ROT_QUARTER
    row_lane = lane % HEAD_DIM < HEAD_DIM // 2

    def table(row_ref, col_ref):
        col = col_ref[...]
        return jnp.concatenate(
            [jnp.where(row_lane, jnp.broadcast_to(row_ref[j:j + 1, :], col.shape), col)
             for j in range(row_ref.shape[0])], axis=0)

    cos = table(cos_row_ref, cos_col_ref)
    sin = table(sin_row_ref, sin_col_ref)
    qk = proj(OFF_RET, 2 * RET_DIM)
    for slab in range(2 * RET_DIM // LANES):
        lo = LANES * slab
        x = qk[:, lo:lo + LANES]
        partner = jnp.where(first, pltpu.roll(x, LANES - ROT_QUARTER, 1), pltpu.roll(x, ROT_QUARTER, 1))
        x = x * cos + partner * sin
        if lo >= RET_DIM:
            x = x * HEAD_DIM ** -0.5
        ret_ref[:, lo:lo + LANES] = x.astype(bf16)
    vg = proj(OFF_RET + 2 * RET_DIM, 2 * RET_DIM)
    ret_ref[:, 2 * RET_DIM:] = vg[:, :RET_DIM].astype(bf16)
    gate_ref[...] = vg[:, RET_DIM:]


def in_proj(h, g, mod, w_in, rope, cast=()):
    L = h.shape[0]
    tm = _row_tile(L)
    steps = L // tm
    assert tm % GRID_W == 0
    cast_in, cast_out, cast_shapes = [], [], []
    for stack, layer in cast:
        _, k, n = stack.shape
        nblk = max(b for b in range(1, steps + 1) if steps % b == 0 and k % (16 * b) == 0)
        block = functools.partial(lambda i, nblk: i * nblk // steps, nblk=nblk)
        cast_in.append(pl.BlockSpec((None, k // nblk, n),
                                    functools.partial(lambda i, layer, block: (layer, block(i), 0), layer=layer, block=block)))
        cast_out.append(pl.BlockSpec((k // nblk, n), functools.partial(lambda i, block: (block(i), 0), block=block)))
        cast_shapes.append(jax.ShapeDtypeStruct((k, n), bf16))
    row = lambda n: pl.BlockSpec((tm, n), lambda i: (i, 0))
    full = lambda a: pl.BlockSpec(a.shape, lambda i: (0,) * a.ndim)
    grid_rows = pl.BlockSpec((tm // GRID_W, LANES), lambda i: (i, 0))
    grid_cols = pl.BlockSpec((GRID_W, LANES), lambda i: (0, 0))
    return pl.pallas_call(
        _in_proj_kernel,
        grid=(L // tm,),
        in_specs=[row(D_MODEL), full(g), full(mod), full(w_in),
                  grid_rows, grid_rows, grid_cols, grid_cols] + cast_in,
        out_specs=[row(POOL_DIM), row(3 * NA_DIM), row(3 * RET_DIM), row(RET_DIM)] + cast_out,
        out_shape=[
            jax.ShapeDtypeStruct((L, POOL_DIM), f32),
            jax.ShapeDtypeStruct((L, 3 * NA_DIM), bf16),
            jax.ShapeDtypeStruct((L, 3 * RET_DIM), bf16),
            jax.ShapeDtypeStruct((L, RET_DIM), f32),
        ] + cast_shapes,
        compiler_params=_params("arbitrary"),
        name="in_proj",
    )(h, g, mod, w_in, *rope, *[stack for stack, _ in cast])


def rope_tables(rows):
    inv = ROPE_BASE ** (-jnp.arange(ROT_QUARTER, dtype=f32) / ROT_QUARTER)

    def factors(n):
        ang = jnp.arange(n, dtype=f32)[:, None] * inv[None, :]
        cos = jnp.tile(jnp.cos(ang), (1, LANES // ROT_QUARTER))
        sin = jnp.tile(jnp.concatenate([-jnp.sin(ang), jnp.sin(ang)], axis=-1), (1, LANES // (2 * ROT_QUARTER)))
        return cos, sin

    return factors(rows) + factors(GRID_W)


def rope_identity(rows):
    ones = lambda n: jnp.ones((n, LANES), f32)
    zeros = lambda n: jnp.zeros((n, LANES), f32)
    return ones(rows), zeros(rows), ones(GRID_W), zeros(GRID_W)


POOL_HALO = 8
POOL_MARGIN = 8


def _pooled_rows(prev, x_ref, nxt, w_ref, scale_ref, lvl_ref, *, first, last, row0, L):
    assert POOL_WINDOWS == (2, 4, 8, 16)
    tm = x_ref.shape[0]
    halo = prev.shape[0]
    data = tm + 2 * halo
    m0 = POOL_MARGIN
    zeros = jnp.zeros((m0, POOL_DIM), f32)
    for k in range(3):
        lvl_ref[k, 0:m0, :] = zeros
        lvl_ref[k, m0 + data:, :] = zeros
    lvl_ref[0, m0:m0 + halo, :] = jnp.where(first, 0.0, prev)
    lvl_ref[0, m0 + halo:m0 + halo + tm, :] = x_ref[...]
    lvl_ref[0, m0 + halo + tm:m0 + data, :] = jnp.where(last, 0.0, nxt)

    def shifted(k, d, rows=data, start=m0):
        return lvl_ref[k, start + d:start + d + rows, :]

    lvl_ref[1, m0:m0 + data, :] = shifted(0, -1) + shifted(0, 0)
    lvl_ref[2, m0:m0 + data, :] = shifted(1, -1) + shifted(1, 1)
    lvl_ref[3, m0:m0 + data, :] = shifted(2, -2) + shifted(2, 2)
    out_rows = data - 2 * POOL_HALO
    centre = m0 + POOL_HALO
    x = shifted(0, 0, out_rows, centre)
    sums = [shifted(k, 0, out_rows, centre) for k in (1, 2, 3)]
    sums.append(shifted(3, -4, out_rows, centre) + shifted(3, 4, out_rows, centre))
    lane = lax.broadcasted_iota(jnp.int32, (1, POOL_DIM), 1)
    gdim = POOL_DIM // POOL_GROUPS
    total = sums[-1]
    half = jnp.full((1, POOL_DIM), POOL_WINDOWS[-1] // 2, jnp.int32)
    for gi in range(POOL_GROUPS - 2, -1, -1):
        sel = lane < gdim * (gi + 1)
        total = jnp.where(sel, sums[gi], total)
        half = jnp.where(sel, POOL_WINDOWS[gi] // 2, half)
    edge = 4 * POOL_HALO
    assert halo <= edge <= out_rows // 2

    def clipped_mean(lo):
        t = row0 - (halo - POOL_HALO) + lo + lax.broadcasted_iota(jnp.int32, (edge, 1), 0)
        count = jnp.maximum(jnp.minimum(t + half, L) - jnp.maximum(t - half, 0), 1)
        return total[lo:lo + edge] / count.astype(f32)

    mean = total[edge:out_rows - edge] * (0.5 / half.astype(f32))
    mean = jnp.concatenate([clipped_mean(0), mean, clipped_mean(out_rows - edge)], axis=0)
    m = (mean - x).astype(bf16)
    return (jnp.dot(m, w_ref[...], preferred_element_type=f32) * scale_ref[...]).astype(bf16)


def pool_blockdiag(pool_w):
    gdim = POOL_DIM // POOL_GROUPS
    w = jnp.zeros((POOL_DIM, POOL_DIM), f32)
    for g in range(POOL_GROUPS):
        w = w.at[g * gdim:(g + 1) * gdim, g * gdim:(g + 1) * gdim].set(pool_w[g])
    return w.astype(bf16)


NA_RB = 4
NA_QT = NA_RB * GRID_W
NA_WIN = NA_RB + NA_KH
NA_KB = NA_WIN // NA_RB
NA_DR_MASKED = 2 * NA_KH - 1
NA_VARIANTS = 3


def na_bias_rows(rpb):
    assert LANES == 2 * GRID_W
    w = jnp.zeros(rpb.shape[:-1] + (LANES,), f32)
    w = w.at[..., :NA_KW].set(rpb[..., NA_KW - 1:])
    w = w.at[..., LANES - (NA_KW - 1):].set(rpb[..., :NA_KW - 1])
    return w * LOG2E


def _na_fill_bias(rows_ref, bias_scr, dr_maps):
    lane = lax.broadcasted_iota(jnp.int32, (GRID_W, LANES), 1)
    qcol = lax.broadcasted_iota(jnp.int32, (GRID_W, LANES), 0)
    kcol = lane % GRID_W
    ws = jnp.clip(qcol - NA_KW // 2, 0, GRID_W - NA_KW)
    in_window = jnp.logical_and(kcol >= ws, kcol < ws + NA_KW)
    masked = jnp.full((GRID_W, LANES), NEG_INF, f32)
    for h in range(NA_HEADS):
        for dr in range(NA_DR_MASKED + 1):
            if dr == NA_DR_MASKED:
                tile = masked
            else:
                row = jnp.broadcast_to(rows_ref[h, dr:dr + 1, :], (GRID_W, LANES))
                t = pltpu.roll(row, 0, 1, stride=1, stride_axis=0)
                t = jnp.where(lane < GRID_W, t, pltpu.roll(t, GRID_W, 1))
                tile = jnp.where(in_window, t, NEG_INF)
            for var in range(NA_VARIANTS):
                for j, i in zip(*np.nonzero(dr_maps[var] == dr)):
                    half = GRID_W * (int(i) % 2)
                    bias_scr[var, h, GRID_W * j:GRID_W * (j + 1), GRID_W * i:GRID_W * (i + 1)] = (
                        tile[:, half:half + GRID_W])


def _na_key_base(g, nb):
    return jnp.clip(g - 1, 0, nb - NA_KB)


def _na_dr_maps(rows):
    nb = rows // NA_RB
    maps = []
    for g in range(nb):
        kb = NA_RB * min(max(g - 1, 0), nb - NA_KB)
        m = np.full((NA_RB, NA_WIN), NA_DR_MASKED)
        for j in range(NA_RB):
            r = NA_RB * g + j
            rs = min(max(r - NA_KH // 2, 0), rows - NA_KH)
            for i in range(rs - kb, rs - kb + NA_KH):
                m[j, i] = kb + i - r + NA_KH - 1
        maps.append(m)
    variants = [maps[0], maps[1], maps[nb - 1]]
    for g in range(nb):
        assert (maps[g] == variants[0 if g == 0 else 2 if g == nb - 1 else 1]).all()
    return variants


def _per_head_rows(x, lo_half):
    zero = jnp.zeros_like(x)
    return jnp.concatenate([jnp.where(lo_half, x, zero), jnp.where(lo_half, zero, x)], axis=0)


def _with_ones(v, lo_half, sub):
    ones = jnp.ones_like(v)
    return jnp.where(lo_half, v, ones) if sub == 0 else jnp.where(lo_half, ones, v)


def _softmax_pv(s_parts, v_parts):
    maxes = [jnp.max(s, axis=-1, keepdims=True) for s in s_parts]
    m = functools.reduce(jnp.maximum, maxes)
    out = None
    for s, v, mk in zip(s_parts, v_parts, maxes):
        pv = jnp.dot(jnp.exp2(s - mk).astype(bf16), v, preferred_element_type=f32) * jnp.exp2(mk - m)
        out = pv if out is None else out + pv
    return out / pltpu.roll(out, HEAD_DIM, 1)


def _na_kernel(q_ref, k0_ref, k1_ref, k2_ref, v0_ref, v1_ref, v2_ref, kc_ref, vc_ref, rows_ref, o_ref,
               bias_scr, *, nb, dr_maps):
    g = pl.program_id(0)

    @pl.when(g == 0)
    def _():
        _na_fill_bias(rows_ref, bias_scr, dr_maps)

    var = jnp.where(g == 0, 0, jnp.where(g == nb - 1, 2, 1))
    lo_half = _lo_half()
    k_refs = (k0_ref, k1_ref, k2_ref)
    v_refs = (v0_ref, v1_ref, v2_ref)
    for hp in range(N_PAIRS):
        sl = slice(LANES * hp, LANES * (hp + 1))
        q2 = q_ref[:, sl]
        values = [vr[:, sl] for vr in v_refs] + [vc_ref[:, sl]]
        outs = []
        for sub in range(2):
            values_sub = [_with_ones(v, lo_half, sub) for v in values]
            qm = jnp.where(lo_half if sub == 0 else jnp.logical_not(lo_half), q2, jnp.zeros_like(q2))
            scores = []
            for kb in range(NA_KB):
                s = lax.dot_general(qm, k_refs[kb][:, sl], NT_DIMS, preferred_element_type=f32)
                tb = bias_scr[var, 2 * hp + sub, :, NA_QT * kb:NA_QT * (kb + 1)]
                scores.append(jnp.where(tb < 0.5 * NEG_INF, NEG_INF, s + tb))
            scores.append(lax.dot_general(qm, kc_ref[:, sl], NT_DIMS, preferred_element_type=f32))
            outs.append(_softmax_pv(scores, values_sub))
        o_ref[:, sl] = jnp.where(lo_half, outs[0], outs[1]).astype(bf16)


def neighborhood_attention(na, na_ctx, bias_rows):
    L = na.shape[0]
    rows = L // GRID_W
    nb = rows // NA_RB
    assert rows % NA_RB == 0 and nb > NA_KB
    ctx = na_ctx.shape[0]
    blk = lambda off, col: pl.BlockSpec((NA_QT, NA_DIM), lambda g: (_na_key_base(g, nb) + off, col))
    return pl.pallas_call(
        functools.partial(_na_kernel, nb=nb, dr_maps=_na_dr_maps(rows)),
        grid=(nb,),
        in_specs=[
            pl.BlockSpec((NA_QT, NA_DIM), lambda g: (g, 0)),
            blk(0, 1), blk(1, 1), blk(2, 1),
            blk(0, 2), blk(1, 2), blk(2, 2),
            pl.BlockSpec((ctx, NA_DIM), lambda g: (0, 1)),
            pl.BlockSpec((ctx, NA_DIM), lambda g: (0, 2)),
            pl.BlockSpec(bias_rows.shape, lambda g: (0, 0, 0)),
        ],
        out_specs=pl.BlockSpec((NA_QT, NA_DIM), lambda g: (g, 0)),
        out_shape=jax.ShapeDtypeStruct((L, NA_DIM), bf16),
        scratch_shapes=[pltpu.VMEM((NA_VARIANTS, NA_HEADS, NA_QT, NA_WIN * GRID_W), f32)],
        compiler_params=_params("arbitrary"),
        name="neighborhood_attention",
    )(na, na, na, na, na, na, na, na_ctx, na_ctx, bias_rows)


def _ctx_attn_kernel(q_ref, k_ref, v_ref, o_ref):
    lo_half = _lo_half()
    for hp in range(N_PAIRS):
        sl = slice(LANES * hp, LANES * (hp + 1))
        q2 = q_ref[:, sl]
        k2 = k_ref[:, sl]
        v2 = v_ref[:, sl]
        outs = []
        for sub in range(2):
            qm = jnp.where(lo_half if sub == 0 else jnp.logical_not(lo_half), q2, jnp.zeros_like(q2))
            s = lax.dot_general(qm, k2, NT_DIMS, preferred_element_type=f32)
            outs.append(_softmax_pv([s], [_with_ones(v2, lo_half, sub)]))
        o_ref[:, sl] = jnp.where(lo_half, outs[0], outs[1]).astype(bf16)


def context_attention(na_ctx):
    ctx = na_ctx.shape[0]
    col = lambda c: pl.BlockSpec((ctx, NA_DIM), lambda i: (0, c))
    return pl.pallas_call(
        _ctx_attn_kernel,
        grid=(1,),
        in_specs=[col(0), col(1), col(2)],
        out_specs=pl.BlockSpec((ctx, NA_DIM), lambda i: (0, 0)),
        out_shape=jax.ShapeDtypeStruct((ctx, NA_DIM), bf16),
        compiler_params=_params("arbitrary"),
        name="context_attention",
    )(na_ctx, na_ctx, na_ctx)


RET_STEP = 16


def _pair_blockmask():
    r = lax.broadcasted_iota(jnp.int32, (LANES, LANES), 0) < HEAD_DIM
    c = lax.broadcasted_iota(jnp.int32, (LANES, LANES), 1) < HEAD_DIM
    return r == c


def _ret_state_kernel(k_ref, v_ref, lgf_ref, lgb_ref, s0f_ref, s0b_ref, sf_out, sb_out, sf_fin, sb_fin, *, n_chunks):
    c = lax.broadcasted_iota(jnp.int32, (RET_CHUNK, 1), 0).astype(f32)
    blockmask = _pair_blockmask()
    lgf = lgf_ref[...]
    lgb = lgb_ref[...]
    kdec_f = jnp.exp((RET_CHUNK - 1.0 - c) * lgf)
    kdec_b = jnp.exp(c * lgb)
    cdec_f = jnp.exp(float(RET_CHUNK) * lgf)
    cdec_b = jnp.exp(float(RET_CHUNK) * lgb)

    def advance(state, chunk, kdec, cdec, out):
        rows = pl.ds(pl.multiple_of(chunk * RET_CHUNK, RET_CHUNK), RET_CHUNK)
        out[chunk] = state.astype(bf16)
        kd = (k_ref[rows, :].astype(f32) * kdec).astype(bf16)
        kv = lax.dot_general(kd, v_ref[rows, :], TN_DIMS, preferred_element_type=f32)
        return state * cdec + jnp.where(blockmask, kv, 0.0)

    def body(n, carry):
        sf, sb = carry
        return (advance(sf, n, kdec_f, cdec_f, sf_out), advance(sb, n_chunks - 1 - n, kdec_b, cdec_b, sb_out))

    sf, sb = lax.fori_loop(0, n_chunks, body, (s0f_ref[...], s0b_ref[...]), unroll=min(n_chunks, 8))
    sf_fin[...] = sf
    sb_fin[...] = sb


def retention_states(ret, lgf_lane, lgb_lane, s0f, s0b):
    L = ret.shape[0]
    N = L // RET_CHUNK
    col = lambda c: pl.BlockSpec((L, LANES), lambda hp: (0, c * N_PAIRS + hp))
    vec = pl.BlockSpec((1, LANES), lambda hp: (0, hp))
    st = pl.BlockSpec((None, LANES, LANES), lambda hp: (hp, 0, 0))
    per_chunk = pl.BlockSpec((N, None, LANES, LANES), lambda hp: (0, hp, 0, 0))
    st_shape = jax.ShapeDtypeStruct((N_PAIRS, LANES, LANES), f32)
    all_shape = jax.ShapeDtypeStruct((N, N_PAIRS, LANES, LANES), bf16)
    return pl.pallas_call(
        functools.partial(_ret_state_kernel, n_chunks=N),
        grid=(N_PAIRS,),
        in_specs=[col(1), col(2), vec, vec, st, st],
        out_specs=[per_chunk, per_chunk, st, st],
        out_shape=[all_shape, all_shape, st_shape, st_shape],
        compiler_params=_params("parallel"),
        name="retention_states",
    )(ret, ret, lgf_lane, lgb_lane, s0f, s0b)


def _ret_out_kernel(lgf_s, lgb_s, q_ref, k_ref, v_ref, gate_ref, sf_ref, sb_ref, lgf_ref, lgb_ref, gn_ref, o_ref,
                    *, cs):
    lo_half = _lo_half()
    diff = (lax.broadcasted_iota(jnp.int32, (RET_CHUNK, RET_CHUNK), 0)
            - lax.broadcasted_iota(jnp.int32, (RET_CHUNK, RET_CHUNK), 1)).astype(f32)
    c = lax.broadcasted_iota(jnp.int32, (RET_CHUNK, 1), 0).astype(f32)
    qdec_f = jnp.exp((c + 1.0) * lgf_ref[...])
    qdec_b = jnp.exp((RET_CHUNK - c) * lgb_ref[...])
    decay = [jnp.where(diff >= 0, jnp.exp(diff * lgf_s[h]), 0.0) + jnp.where(diff <= 0, jnp.exp(-diff * lgb_s[h]), 0.0)
             for h in range(RET_HEADS)]
    inv_hd = 1.0 / HEAD_DIM

    def head_mean(t):
        m0 = jnp.sum(jnp.where(lo_half, t, 0.0), axis=-1, keepdims=True) * inv_hd
        m1 = jnp.sum(jnp.where(lo_half, 0.0, t), axis=-1, keepdims=True) * inv_hd
        return jnp.where(lo_half, m0, m1)

    for ci in range(cs):
        rows = slice(RET_CHUNK * ci, RET_CHUNK * (ci + 1))
        q = q_ref[rows, :]
        qf32 = q.astype(f32)
        q_f = (qf32 * qdec_f).astype(bf16)
        q_b = (qf32 * qdec_b).astype(bf16)
        for hp in range(N_PAIRS):
            sl = slice(LANES * hp, LANES * (hp + 1))
            q2 = q[:, sl]
            k2 = k_ref[rows, sl]
            v2 = v_ref[rows, sl]
            a = lax.dot_general(q2, _per_head_rows(k2, lo_half), NT_DIMS, preferred_element_type=f32)
            lhs = [(a[:, :RET_CHUNK] * decay[2 * hp]).astype(bf16), (a[:, RET_CHUNK:] * decay[2 * hp + 1]).astype(bf16),
                   q_f[:, sl], q_b[:, sl]]
            rhs = [_per_head_rows(v2, lo_half), sf_ref[ci, hp], sb_ref[ci, hp]]
            y = jnp.dot(jnp.concatenate(lhs, axis=1), jnp.concatenate(rhs, axis=0), preferred_element_type=f32)
            d = y - head_mean(y)
            yn = d * lax.rsqrt(head_mean(d * d) + EPS) * gn_ref[:, sl]
            o_ref[rows, sl] = (_silu(gate_ref[rows, sl]) * yn).astype(bf16)


def retention_outputs(ret, gate, sf, sb, lgf, lgb, lgf_lane, lgb_lane, gn):
    L = ret.shape[0]
    N = L // RET_CHUNK
    cs = min(RET_STEP, N)
    col = lambda c: pl.BlockSpec((cs * RET_CHUNK, RET_DIM), lambda n, *_: (n, c))
    vec = pl.BlockSpec((1, RET_DIM), lambda n, *_: (0, 0))
    st = pl.BlockSpec((cs, N_PAIRS, LANES, LANES), lambda n, *_: (n, 0, 0, 0))
    return pl.pallas_call(
        functools.partial(_ret_out_kernel, cs=cs),
        grid_spec=pltpu.PrefetchScalarGridSpec(
            num_scalar_prefetch=2,
            grid=(N // cs,),
            in_specs=[col(0), col(1), col(2), col(0), st, st, vec, vec, vec],
            out_specs=pl.BlockSpec((cs * RET_CHUNK, RET_DIM), lambda n, *_: (n, 0)),
        ),
        out_shape=jax.ShapeDtypeStruct((L, RET_DIM), bf16),
        compiler_params=_params("parallel"),
        name="retention_outputs",
    )(lgf, lgb, ret, ret, ret, gate, sf, sb, lgf_lane, lgb_lane, gn)


FFN_HALO = 16
FFN_TF = 256
POOL_IN_HALO = FFN_HALO + POOL_HALO
POOL_IN_BLOCK = 2 * FFN_HALO


def _out_ffn_kernel(hp_ref, h_ref, hn_ref, pp_ref, p_ref, pn_ref, np_ref, n_ref, nn_ref, rp_ref, r_ref, rn_ref,
                    mod_ref, g_ref, wp_ref, ps_ref, wo_ref, wu_ref, cw_ref, cb_ref, wd_ref, fg_ref, o_ref,
                    aext, gbuf, lvl, *, tm, L, final):
    i = pl.program_id(0)
    ni = pl.num_programs(0)
    ext = tm + 2 * FFN_HALO

    def rows3(prev, cur, nxt):
        return jnp.concatenate([prev[...], cur[...], nxt[...]], axis=0)

    y = jnp.dot(jnp.concatenate([rows3(np_ref, n_ref, nn_ref), rows3(rp_ref, r_ref, rn_ref)], axis=1),
                wo_ref[POOL_DIM:, :], preferred_element_type=f32)
    pool_x = _pooled_rows(pp_ref[POOL_IN_BLOCK - POOL_IN_HALO:, :], p_ref, pn_ref[:POOL_IN_HALO, :], wp_ref, ps_ref,
                          lvl, first=i == 0, last=i == ni - 1, row0=i * tm, L=L)
    y = y + jnp.dot(pool_x, wo_ref[:POOL_DIM, :], preferred_element_type=f32)
    hx = rows3(hp_ref, h_ref, hn_ref) + mod_ref[2:3, :] * y
    o_ref[...] = hx[FFN_HALO:FFN_HALO + tm]
    a = hx * lax.rsqrt(jnp.mean(hx * hx, axis=-1, keepdims=True) + EPS) * g_ref[...]
    a = a * (1.0 + mod_ref[4:5, :]) + mod_ref[3:4, :]
    row = lax.broadcasted_iota(jnp.int32, (ext, 1), 0)
    inside = jnp.logical_and(jnp.logical_or(row >= FFN_HALO, i > 0), jnp.logical_or(row < FFN_HALO + tm, i < ni - 1))
    aext[...] = jnp.where(inside, a, 0.0).astype(bf16)
    ae = aext[...]

    def branch(lo):
        u = jnp.dot(ae, wu_ref[:, lo:lo + FFN_TF], preferred_element_type=f32)
        up = pltpu.roll(u, 1, 0)[FFN_HALO:FFN_HALO + tm]
        un = pltpu.roll(u, ext - 1, 0)[FFN_HALO:FFN_HALO + tm]
        uc = u[FFN_HALO:FFN_HALO + tm]
        cw = cw_ref[:, lo:lo + FFN_TF]
        return up * cw[0:1, :] + uc * cw[1:2, :] + un * cw[2:3, :] + cb_ref[:, lo:lo + FFN_TF]

    for s in range(D_FF // FFN_TF):
        lo = FFN_TF * s
        gbuf[:, lo:lo + FFN_TF] = (_silu(branch(lo)) * branch(D_FF + lo)).astype(bf16)
    y = jnp.dot(gbuf[...], wd_ref[...], preferred_element_type=f32)
    out = o_ref[...] + mod_ref[5:6, :] * y
    if final:
        out = out * lax.rsqrt(jnp.mean(out * out, axis=-1, keepdims=True) + EPS) * fg_ref[...]
    o_ref[...] = out


def out_proj_ffn(h, p, na_x, ret_x, mod, g2, w_pool, pool_scale, w_out, w_up, conv_w, conv_b, w_down, final_g, final):
    L = h.shape[0]
    tm = _row_tile(L)

    def halo3(n, halo=FFN_HALO):
        hb = tm // halo
        return [
            pl.BlockSpec((halo, n), lambda i: (jnp.maximum(i * hb - 1, 0), 0)),
            pl.BlockSpec((tm, n), lambda i: (i, 0)),
            pl.BlockSpec((halo, n), lambda i: (jnp.minimum((i + 1) * hb, L // halo - 1), 0)),
        ]

    const = lambda arr: pl.BlockSpec(arr.shape, lambda i: (0,) * arr.ndim, pipeline_mode=pl.Buffered(1))
    consts = (mod, g2, w_pool, pool_scale, w_out, w_up, conv_w, conv_b, w_down, final_g)
    return pl.pallas_call(
        functools.partial(_out_ffn_kernel, tm=tm, L=L, final=final),
        grid=(L // tm,),
        in_specs=(halo3(D_MODEL) + halo3(POOL_DIM, POOL_IN_BLOCK) + halo3(NA_DIM) + halo3(RET_DIM)
                  + [const(a) for a in consts]),
        out_specs=pl.BlockSpec((tm, D_MODEL), lambda i: (i, 0)),
        out_shape=jax.ShapeDtypeStruct((L, D_MODEL), f32),
        scratch_shapes=[pltpu.VMEM((tm + 2 * FFN_HALO, D_MODEL), bf16), pltpu.VMEM((tm, D_FF), bf16),
                        pltpu.VMEM((4, tm + 2 * (POOL_IN_HALO + POOL_MARGIN), POOL_DIM), f32)],
        compiler_params=_params("parallel"),
        name="out_proj_ffn",
    )(h, h, h, p, p, p, na_x, na_x, na_x, ret_x, ret_x, ret_x, *consts)


def kernel(x, c, ctx, c_ctx, w_mod, b_mod, norm1_g, w_in, pool_w, pool_scale, na_rpb, ret_decay_fwd, ret_decay_bwd,
           ret_gn_g, w_out, norm2_g, w_up, conv_w, conv_b, w_down, final_g):
    B, L, _ = x.shape
    assert B == 1
    n_ctx = ctx.shape[1]
    h = x[0]
    hc = ctx[0]
    mods = mod_vectors(c, c_ctx, w_mod, b_mod)
    rope = rope_tables(L // GRID_W)
    no_rope = rope_identity(n_ctx // GRID_W)
    zero_state = jnp.zeros((N_PAIRS, LANES, LANES), f32)
    fg = final_g.reshape(1, D_MODEL)
    stacks = (w_in, w_out, w_up, w_down)
    wi, wo, wu, wd = [stack[0].astype(bf16) for stack in stacks]
    bias_rows = na_bias_rows(na_rpb)
    for l in range(DEPTH):
        need_ctx = l < DEPTH - 1
        mx = mods[l, 0].reshape(6, D_MODEL)
        mc = mods[l, 1].reshape(6, D_MODEL)
        g1 = norm1_g[l].reshape(1, D_MODEL)
        g2 = norm2_g[l].reshape(1, D_MODEL)
        cb = conv_b[l].reshape(1, 2 * D_FF)
        wp = pool_blockdiag(pool_w[l])
        ps = pool_scale[l].reshape(1, POOL_DIM)
        lgf = ret_decay_fwd[l].astype(f32)
        lgb = ret_decay_bwd[l].astype(f32)
        lgf_lane = jnp.repeat(lgf, HEAD_DIM).reshape(1, RET_DIM)
        lgb_lane = jnp.repeat(lgb, HEAD_DIM).reshape(1, RET_DIM)
        gn = ret_gn_g[l].reshape(1, RET_DIM)
        ffn_w = (g2, wp, ps, wo, wu, conv_w[l], cb, wd, fg)

        pc, nac, retc, gatec = in_proj(hc, g1, mc, wi, no_rope)
        cast = [(stack, l + 1) for stack in stacks] if need_ctx else []
        px, nax, retx, gatex, *next_w = in_proj(h, g1, mx, wi, rope, cast)

        sfc, sbc, sf0, sb0 = retention_states(retc, lgf_lane, lgb_lane, zero_state, zero_state)
        sfx, sbx, _, _ = retention_states(retx, lgf_lane, lgb_lane, sf0, sb0)

        na_x = neighborhood_attention(nax, nac, bias_rows[l])
        ret_x = retention_outputs(retx, gatex, sfx, sbx, lgf, lgb, lgf_lane, lgb_lane, gn)
        h = out_proj_ffn(h, px, na_x, ret_x, mx, *ffn_w, final=not need_ctx)

        if need_ctx:
            na_c = context_attention(nac)
            ret_c = retention_outputs(retc, gatec, sfc, sbc, lgf, lgb, lgf_lane, lgb_lane, gn)
            hc = out_proj_ffn(hc, pc, na_c, ret_c, mc, *ffn_w, final=False)
            wi, wo, wu, wd = next_w
    return h[None]
```

```python
import functools

import numpy as np
import jax
import jax.numpy as jnp
from jax import lax
from jax.experimental import pallas as pl
from jax.experimental.pallas import tpu as pltpu

f32 = jnp.float32
bf16 = jnp.bfloat16

D_MODEL = 1024
DEPTH = 4
GRID_W = 64
HEAD_DIM = 64
POOL_GROUPS = 4
POOL_WINDOWS = (2, 4, 8, 16)
POOL_DIM = 256
NA_HEADS = 6
NA_DIM = 384
NA_KH = 8
NA_KW = 16
RET_HEADS = 6
RET_DIM = 384
RET_CHUNK = 128
D_FF = 2816
ROPE_BASE = 10000.0
ROT_QUARTER = HEAD_DIM // 4
EPS = 1e-6
NEG_INF = -1e30
IN_DIM = POOL_DIM + 3 * NA_DIM + 4 * RET_DIM
OFF_NA = POOL_DIM
OFF_RET = OFF_NA + 3 * NA_DIM
OFF_GATE = OFF_RET + 3 * RET_DIM

LANES = 128
N_PAIRS = NA_HEADS // 2
VMEM_LIMIT = 56 * 1024 * 1024

LOG2E = float(np.log2(np.e))
NT_DIMS = (((1,), (1,)), ((), ()))
TN_DIMS = (((0,), (0,)), ((), ()))


def _params(*sem, fuse_inputs=None):
    return pltpu.CompilerParams(dimension_semantics=sem, vmem_limit_bytes=VMEM_LIMIT, allow_input_fusion=fuse_inputs)


def _row_tile(L):
    return min(L, 512)


def _silu(x):
    return x * jax.nn.sigmoid(x)


def _lo_half():
    return lax.broadcasted_iota(jnp.int32, (1, LANES), 1) < HEAD_DIM


def _mod_kernel(s_ref, w_ref, b_ref, o_ref):
    a = _silu(s_ref[...]).astype(bf16)
    o_ref[...] = jnp.dot(a, w_ref[...].astype(bf16), preferred_element_type=f32) + b_ref[...]


MOD_BLOCK = 3 * D_MODEL


def mod_vectors(c, c_ctx, w_mod, b_mod):
    s = jnp.zeros((8, D_MODEL), f32).at[0].set(c[0]).at[1].set(c_ctx)
    return pl.pallas_call(
        _mod_kernel,
        grid=(DEPTH, 6 * D_MODEL // MOD_BLOCK),
        in_specs=[
            pl.BlockSpec((8, D_MODEL), lambda l, j: (0, 0)),
            pl.BlockSpec((None, D_MODEL, MOD_BLOCK), lambda l, j: (l, 0, j)),
            pl.BlockSpec((None, 1, MOD_BLOCK), lambda l, j: (l, 0, j)),
        ],
        out_specs=pl.BlockSpec((None, 8, MOD_BLOCK), lambda l, j: (l, 0, j)),
        out_shape=jax.ShapeDtypeStruct((DEPTH, 8, 6 * D_MODEL), f32),
        compiler_params=_params("parallel", "parallel"),
        name="mod_vectors",
    )(s, w_mod, b_mod.reshape(DEPTH, 1, 6 * D_MODEL))


def _in_proj_kernel(h_ref, g_ref, mod_ref, w_ref, cos_row_ref, sin_row_ref, cos_col_ref, sin_col_ref, *refs):
    n_cast = (len(refs) - 4) // 2
    cast_in = refs[:n_cast]
    p_ref, na_ref, ret_ref, gate_ref = refs[n_cast:n_cast + 4]
    for src, dst in zip(cast_in, refs[n_cast + 4:]):
        dst[...] = src[...].astype(bf16)
    h = h_ref[...]
    y = h * lax.rsqrt(jnp.mean(h * h, axis=-1, keepdims=True) + EPS) * g_ref[...]
    a = (y * (1.0 + mod_ref[1:2, :]) + mod_ref[0:1, :]).astype(bf16)

    def proj(lo, n):
        return jnp.dot(a, w_ref[:, lo:lo + n], preferred_element_type=f32)

    p_ref[...] = proj(0, POOL_DIM)
    na = proj(OFF_NA, 3 * NA_DIM)
    na_ref[:, 0:NA_DIM] = (na[:, 0:NA_DIM] * (HEAD_DIM ** -0.5 * LOG2E)).astype(bf16)
    na_ref[:, NA_DIM:] = na[:, NA_DIM:].astype(bf16)
    lane = lax.broadcasted_iota(jnp.int32, (1, LANES), 1)
    first = lane % (2 * ROT_QUARTER) < ROT_QUARTER
    row_lane = lane % HEAD_DIM < HEAD_DIM // 2

    def table(row_ref, col_ref):
        col = col_ref[...]
        return jnp.concatenate(
            [jnp.where(row_lane, jnp.broadcast_to(row_ref[j:j + 1, :], col.shape), col)
             for j in range(row_ref.shape[0])], axis=0)

    cos = table(cos_row_ref, cos_col_ref)
    sin = table(sin_row_ref, sin_col_ref)
    qk = proj(OFF_RET, 2 * RET_DIM)
    for slab in range(2 * RET_DIM // LANES):
        lo = LANES * slab
        x = qk[:, lo:lo + LANES]
        partner = jnp.where(first, pltpu.roll(x, LANES - ROT_QUARTER, 1), pltpu.roll(x, ROT_QUARTER, 1))
        x = x * cos + partner * sin
        if lo >= RET_DIM:
            x = x * HEAD_DIM ** -0.5
        ret_ref[:, lo:lo + LANES] = x.astype(bf16)
    vg = proj(OFF_RET + 2 * RET_DIM, 2 * RET_DIM)
    ret_ref[:, 2 * RET_DIM:] = vg[:, :RET_DIM].astype(bf16)
    gate_ref[...] = vg[:, RET_DIM:]


def in_proj(h, g, mod, w_in, rope, cast=()):
    L = h.shape[0]
    tm = _row_tile(L)
    steps = L // tm
    assert tm % GRID_W == 0
    cast_in, cast_out, cast_shapes = [], [], []
    for stack, layer in cast:
        _, k, n = stack.shape
        nblk = max(b for b in range(1, steps + 1) if steps % b == 0 and k % (16 * b) == 0)
        block = functools.partial(lambda i, nblk: i * nblk // steps, nblk=nblk)
        cast_in.append(pl.BlockSpec((None, k // nblk, n),
                                    functools.partial(lambda i, layer, block: (layer, block(i), 0), layer=layer, block=block)))
        cast_out.append(pl.BlockSpec((k // nblk, n), functools.partial(lambda i, block: (block(i), 0), block=block)))
        cast_shapes.append(jax.ShapeDtypeStruct((k, n), bf16))
    row = lambda n: pl.BlockSpec((tm, n), lambda i: (i, 0))
    full = lambda a: pl.BlockSpec(a.shape, lambda i: (0,) * a.ndim)
    grid_rows = pl.BlockSpec((tm // GRID_W, LANES), lambda i: (i, 0))
    grid_cols = pl.BlockSpec((GRID_W, LANES), lambda i: (0, 0))
    return pl.pallas_call(
        _in_proj_kernel,
        grid=(L // tm,),
        in_specs=[row(D_MODEL), full(g), full(mod), full(w_in),
                  grid_rows, grid_rows, grid_cols, grid_cols] + cast_in,
        out_specs=[row(POOL_DIM), row(3 * NA_DIM), row(3 * RET_DIM), row(RET_DIM)] + cast_out,
        out_shape=[
            jax.ShapeDtypeStruct((L, POOL_DIM), f32),
            jax.ShapeDtypeStruct((L, 3 * NA_DIM), bf16),
            jax.ShapeDtypeStruct((L, 3 * RET_DIM), bf16),
            jax.ShapeDtypeStruct((L, RET_DIM), f32),
        ] + cast_shapes,
        compiler_params=_params("arbitrary", fuse_inputs=[k == 3 for k in range(8 + len(cast))]),
        name="in_proj",
    )(h, g, mod, w_in, *rope, *[stack for stack, _ in cast])


def rope_tables(rows):
    inv = ROPE_BASE ** (-jnp.arange(ROT_QUARTER, dtype=f32) / ROT_QUARTER)

    def factors(n):
        ang = jnp.arange(n, dtype=f32)[:, None] * inv[None, :]
        cos = jnp.tile(jnp.cos(ang), (1, LANES // ROT_QUARTER))
        sin = jnp.tile(jnp.concatenate([-jnp.sin(ang), jnp.sin(ang)], axis=-1), (1, LANES // (2 * ROT_QUARTER)))
        return cos, sin

    return factors(rows) + factors(GRID_W)


def rope_identity(rows):
    ones = lambda n: jnp.ones((n, LANES), f32)
    zeros = lambda n: jnp.zeros((n, LANES), f32)
    return ones(rows), zeros(rows), ones(GRID_W), zeros(GRID_W)


POOL_HALO = 8
POOL_MARGIN = 8


def _pooled_rows(prev, x_ref, nxt, w_ref, scale_ref, lvl_ref, *, first, last, row0, L):
    assert POOL_WINDOWS == (2, 4, 8, 16)
    tm = x_ref.shape[0]
    halo = prev.shape[0]
    data = tm + 2 * halo
    m0 = POOL_MARGIN
    zeros = jnp.zeros((m0, POOL_DIM), f32)
    for k in range(3):
        lvl_ref[k, 0:m0, :] = zeros
        lvl_ref[k, m0 + data:, :] = zeros
    lvl_ref[0, m0:m0 + halo, :] = jnp.where(first, 0.0, prev)
    lvl_ref[0, m0 + halo:m0 + halo + tm, :] = x_ref[...]
    lvl_ref[0, m0 + halo + tm:m0 + data, :] = jnp.where(last, 0.0, nxt)

    def shifted(k, d, rows=data, start=m0):
        return lvl_ref[k, start + d:start + d + rows, :]

    lvl_ref[1, m0:m0 + data, :] = shifted(0, -1) + shifted(0, 0)
    lvl_ref[2, m0:m0 + data, :] = shifted(1, -1) + shifted(1, 1)
    lvl_ref[3, m0:m0 + data, :] = shifted(2, -2) + shifted(2, 2)
    out_rows = data - 2 * POOL_HALO
    centre = m0 + POOL_HALO
    x = shifted(0, 0, out_rows, centre)
    sums = [shifted(k, 0, out_rows, centre) for k in (1, 2, 3)]
    sums.append(shifted(3, -4, out_rows, centre) + shifted(3, 4, out_rows, centre))
    lane = lax.broadcasted_iota(jnp.int32, (1, POOL_DIM), 1)
    gdim = POOL_DIM // POOL_GROUPS
    total = sums[-1]
    half = jnp.full((1, POOL_DIM), POOL_WINDOWS[-1] // 2, jnp.int32)
    for gi in range(POOL_GROUPS - 2, -1, -1):
        sel = lane < gdim * (gi + 1)
        total = jnp.where(sel, sums[gi], total)
        half = jnp.where(sel, POOL_WINDOWS[gi] // 2, half)
    edge = 4 * POOL_HALO
    assert halo <= edge <= out_rows // 2

    def clipped_mean(lo):
        t = row0 - (halo - POOL_HALO) + lo + lax.broadcasted_iota(jnp.int32, (edge, 1), 0)
        count = jnp.maximum(jnp.minimum(t + half, L) - jnp.maximum(t - half, 0), 1)
        return total[lo:lo + edge] / count.astype(f32)

    mean = total[edge:out_rows - edge] * (0.5 / half.astype(f32))
    mean = jnp.concatenate([clipped_mean(0), mean, clipped_mean(out_rows - edge)], axis=0)
    m = (mean - x).astype(bf16)
    return (jnp.dot(m, w_ref[...], preferred_element_type=f32) * scale_ref[...]).astype(bf16)


def pool_blockdiag(pool_w):
    gdim = POOL_DIM // POOL_GROUPS
    w = jnp.zeros((POOL_DIM, POOL_DIM), f32)
    for g in range(POOL_GROUPS):
        w = w.at[g * gdim:(g + 1) * gdim, g * gdim:(g + 1) * gdim].set(pool_w[g])
    return w.astype(bf16)


NA_RB = 4
NA_QT = NA_RB * GRID_W
NA_WIN = NA_RB + NA_KH
NA_KB = NA_WIN // NA_RB
NA_DR_MASKED = 2 * NA_KH - 1
NA_VARIANTS = 3


def na_bias_rows(rpb):
    assert LANES == 2 * GRID_W
    w = jnp.zeros(rpb.shape[:-1] + (LANES,), f32)
    w = w.at[..., :NA_KW].set(rpb[..., NA_KW - 1:])
    w = w.at[..., LANES - (NA_KW - 1):].set(rpb[..., :NA_KW - 1])
    return w * LOG2E


def _na_fill_bias(rows_ref, bias_scr, dr_maps):
    lane = lax.broadcasted_iota(jnp.int32, (GRID_W, LANES), 1)
    qcol = lax.broadcasted_iota(jnp.int32, (GRID_W, LANES), 0)
    kcol = lane % GRID_W
    ws = jnp.clip(qcol - NA_KW // 2, 0, GRID_W - NA_KW)
    in_window = jnp.logical_and(kcol >= ws, kcol < ws + NA_KW)
    masked = jnp.full((GRID_W, LANES), NEG_INF, f32)
    for h in range(NA_HEADS):
        for dr in range(NA_DR_MASKED + 1):
            if dr == NA_DR_MASKED:
                tile = masked
            else:
                row = jnp.broadcast_to(rows_ref[h, dr:dr + 1, :], (GRID_W, LANES))
                t = pltpu.roll(row, 0, 1, stride=1, stride_axis=0)
                t = jnp.where(lane < GRID_W, t, pltpu.roll(t, GRID_W, 1))
                tile = jnp.where(in_window, t, NEG_INF)
            for var in range(NA_VARIANTS):
                for j, i in zip(*np.nonzero(dr_maps[var] == dr)):
                    half = GRID_W * (int(i) % 2)
                    bias_scr[var, h, GRID_W * j:GRID_W * (j + 1), GRID_W * i:GRID_W * (i + 1)] = (
                        tile[:, half:half + GRID_W])


def _na_key_base(g, nb):
    return jnp.clip(g - 1, 0, nb - NA_KB)


def _na_dr_maps(rows):
    nb = rows // NA_RB
    maps = []
    for g in range(nb):
        kb = NA_RB * min(max(g - 1, 0), nb - NA_KB)
        m = np.full((NA_RB, NA_WIN), NA_DR_MASKED)
        for j in range(NA_RB):
            r = NA_RB * g + j
            rs = min(max(r - NA_KH // 2, 0), rows - NA_KH)
            for i in range(rs - kb, rs - kb + NA_KH):
                m[j, i] = kb + i - r + NA_KH - 1
        maps.append(m)
    variants = [maps[0], maps[1], maps[nb - 1]]
    for g in range(nb):
        assert (maps[g] == variants[0 if g == 0 else 2 if g == nb - 1 else 1]).all()
    return variants


def _per_head_rows(x, lo_half):
    zero = jnp.zeros_like(x)
    return jnp.concatenate([jnp.where(lo_half, x, zero), jnp.where(lo_half, zero, x)], axis=0)


def _with_ones(v, lo_half, sub):
    ones = jnp.ones_like(v)
    return jnp.where(lo_half, v, ones) if sub == 0 else jnp.where(lo_half, ones, v)


def _softmax_pv(s_parts, v_parts):
    maxes = [jnp.max(s, axis=-1, keepdims=True) for s in s_parts]
    m = functools.reduce(jnp.maximum, maxes)
    out = None
    for s, v, mk in zip(s_parts, v_parts, maxes):
        pv = jnp.dot(jnp.exp2(s - mk).astype(bf16), v, preferred_element_type=f32) * jnp.exp2(mk - m)
        out = pv if out is None else out + pv
    return out / pltpu.roll(out, HEAD_DIM, 1)


def _na_kernel(q_ref, k0_ref, k1_ref, k2_ref, v0_ref, v1_ref, v2_ref, kc_ref, vc_ref, rows_ref, o_ref,
               bias_scr, *, nb, dr_maps):
    g = pl.program_id(0)

    @pl.when(g == 0)
    def _():
        _na_fill_bias(rows_ref, bias_scr, dr_maps)

    var = jnp.where(g == 0, 0, jnp.where(g == nb - 1, 2, 1))
    lo_half = _lo_half()
    k_refs = (k0_ref, k1_ref, k2_ref)
    v_refs = (v0_ref, v1_ref, v2_ref)
    for hp in range(N_PAIRS):
        sl = slice(LANES * hp, LANES * (hp + 1))
        q2 = q_ref[:, sl]
        values = [vr[:, sl] for vr in v_refs] + [vc_ref[:, sl]]
        outs = []
        for sub in range(2):
            values_sub = [_with_ones(v, lo_half, sub) for v in values]
            qm = jnp.where(lo_half if sub == 0 else jnp.logical_not(lo_half), q2, jnp.zeros_like(q2))
            scores = []
            for kb in range(NA_KB):
                s = lax.dot_general(qm, k_refs[kb][:, sl], NT_DIMS, preferred_element_type=f32)
                tb = bias_scr[var, 2 * hp + sub, :, NA_QT * kb:NA_QT * (kb + 1)]
                scores.append(jnp.where(tb < 0.5 * NEG_INF, NEG_INF, s + tb))
            scores.append(lax.dot_general(qm, kc_ref[:, sl], NT_DIMS, preferred_element_type=f32))
            outs.append(_softmax_pv(scores, values_sub))
        o_ref[:, sl] = jnp.where(lo_half, outs[0], outs[1]).astype(bf16)


def neighborhood_attention(na, na_ctx, bias_rows):
    L = na.shape[0]
    rows = L // GRID_W
    nb = rows // NA_RB
    assert rows % NA_RB == 0 and nb > NA_KB
    ctx = na_ctx.shape[0]
    blk = lambda off, col: pl.BlockSpec((NA_QT, NA_DIM), lambda g: (_na_key_base(g, nb) + off, col))
    return pl.pallas_call(
        functools.partial(_na_kernel, nb=nb, dr_maps=_na_dr_maps(rows)),
        grid=(nb,),
        in_specs=[
            pl.BlockSpec((NA_QT, NA_DIM), lambda g: (g, 0)),
            blk(0, 1), blk(1, 1), blk(2, 1),
            blk(0, 2), blk(1, 2), blk(2, 2),
            pl.BlockSpec((ctx, NA_DIM), lambda g: (0, 1)),
            pl.BlockSpec((ctx, NA_DIM), lambda g: (0, 2)),
            pl.BlockSpec(bias_rows.shape, lambda g: (0, 0, 0)),
        ],
        out_specs=pl.BlockSpec((NA_QT, NA_DIM), lambda g: (g, 0)),
        out_shape=jax.ShapeDtypeStruct((L, NA_DIM), bf16),
        scratch_shapes=[pltpu.VMEM((NA_VARIANTS, NA_HEADS, NA_QT, NA_WIN * GRID_W), f32)],
        compiler_params=_params("arbitrary"),
        name="neighborhood_attention",
    )(na, na, na, na, na, na, na, na_ctx, na_ctx, bias_rows)


def _ctx_attn_kernel(q_ref, k_ref, v_ref, o_ref):
    lo_half = _lo_half()
    for hp in range(N_PAIRS):
        sl = slice(LANES * hp, LANES * (hp + 1))
        q2 = q_ref[:, sl]
        k2 = k_ref[:, sl]
        v2 = v_ref[:, sl]
        outs = []
        for sub in range(2):
            qm = jnp.where(lo_half if sub == 0 else jnp.logical_not(lo_half), q2, jnp.zeros_like(q2))
            s = lax.dot_general(qm, k2, NT_DIMS, preferred_element_type=f32)
            outs.append(_softmax_pv([s], [_with_ones(v2, lo_half, sub)]))
        o_ref[:, sl] = jnp.where(lo_half, outs[0], outs[1]).astype(bf16)


def context_attention(na_ctx):
    ctx = na_ctx.shape[0]
    col = lambda c: pl.BlockSpec((ctx, NA_DIM), lambda i: (0, c))
    return pl.pallas_call(
        _ctx_attn_kernel,
        grid=(1,),
        in_specs=[col(0), col(1), col(2)],
        out_specs=pl.BlockSpec((ctx, NA_DIM), lambda i: (0, 0)),
        out_shape=jax.ShapeDtypeStruct((ctx, NA_DIM), bf16),
        compiler_params=_params("arbitrary"),
        name="context_attention",
    )(na_ctx, na_ctx, na_ctx)


RET_STEP = 16


def _pair_blockmask():
    r = lax.broadcasted_iota(jnp.int32, (LANES, LANES), 0) < HEAD_DIM
    c = lax.broadcasted_iota(jnp.int32, (LANES, LANES), 1) < HEAD_DIM
    return r == c


def _ret_state_kernel(k_ref, v_ref, lgf_ref, lgb_ref, s0f_ref, s0b_ref, sf_out, sb_out, sf_fin, sb_fin, *, n_chunks):
    c = lax.broadcasted_iota(jnp.int32, (RET_CHUNK, 1), 0).astype(f32)
    blockmask = _pair_blockmask()
    lgf = lgf_ref[...]
    lgb = lgb_ref[...]
    kdec_f = jnp.exp((RET_CHUNK - 1.0 - c) * lgf)
    kdec_b = jnp.exp(c * lgb)
    cdec_f = jnp.exp(float(RET_CHUNK) * lgf)
    cdec_b = jnp.exp(float(RET_CHUNK) * lgb)

    def advance(state, chunk, kdec, cdec, out):
        rows = pl.ds(pl.multiple_of(chunk * RET_CHUNK, RET_CHUNK), RET_CHUNK)
        out[chunk] = state.astype(bf16)
        kd = (k_ref[rows, :].astype(f32) * kdec).astype(bf16)
        kv = lax.dot_general(kd, v_ref[rows, :], TN_DIMS, preferred_element_type=f32)
        return state * cdec + jnp.where(blockmask, kv, 0.0)

    def body(n, carry):
        sf, sb = carry
        return (advance(sf, n, kdec_f, cdec_f, sf_out), advance(sb, n_chunks - 1 - n, kdec_b, cdec_b, sb_out))

    sf, sb = lax.fori_loop(0, n_chunks, body, (s0f_ref[...], s0b_ref[...]), unroll=min(n_chunks, 8))
    sf_fin[...] = sf
    sb_fin[...] = sb


def retention_states(ret, lgf_lane, lgb_lane, s0f, s0b):
    L = ret.shape[0]
    N = L // RET_CHUNK
    col = lambda c: pl.BlockSpec((L, LANES), lambda hp: (0, c * N_PAIRS + hp))
    vec = pl.BlockSpec((1, LANES), lambda hp: (0, hp))
    st = pl.BlockSpec((None, LANES, LANES), lambda hp: (hp, 0, 0))
    per_chunk = pl.BlockSpec((N, None, LANES, LANES), lambda hp: (0, hp, 0, 0))
    st_shape = jax.ShapeDtypeStruct((N_PAIRS, LANES, LANES), f32)
    all_shape = jax.ShapeDtypeStruct((N, N_PAIRS, LANES, LANES), bf16)
    return pl.pallas_call(
        functools.partial(_ret_state_kernel, n_chunks=N),
        grid=(N_PAIRS,),
        in_specs=[col(1), col(2), vec, vec, st, st],
        out_specs=[per_chunk, per_chunk, st, st],
        out_shape=[all_shape, all_shape, st_shape, st_shape],
        compiler_params=_params("parallel"),
        name="retention_states",
    )(ret, ret, lgf_lane, lgb_lane, s0f, s0b)


def _ret_out_kernel(lgf_s, lgb_s, q_ref, k_ref, v_ref, gate_ref, sf_ref, sb_ref, lgf_ref, lgb_ref, gn_ref, o_ref,
                    *, cs):
    lo_half = _lo_half()
    diff = (lax.broadcasted_iota(jnp.int32, (RET_CHUNK, RET_CHUNK), 0)
            - lax.broadcasted_iota(jnp.int32, (RET_CHUNK, RET_CHUNK), 1)).astype(f32)
    c = lax.broadcasted_iota(jnp.int32, (RET_CHUNK, 1), 0).astype(f32)
    qdec_f = jnp.exp((c + 1.0) * lgf_ref[...])
    qdec_b = jnp.exp((RET_CHUNK - c) * lgb_ref[...])
    decay = [jnp.where(diff >= 0, jnp.exp(diff * lgf_s[h]), 0.0) + jnp.where(diff <= 0, jnp.exp(-diff * lgb_s[h]), 0.0)
             for h in range(RET_HEADS)]
    inv_hd = 1.0 / HEAD_DIM

    def head_mean(t):
        m0 = jnp.sum(jnp.where(lo_half, t, 0.0), axis=-1, keepdims=True) * inv_hd
        m1 = jnp.sum(jnp.where(lo_half, 0.0, t), axis=-1, keepdims=True) * inv_hd
        return jnp.where(lo_half, m0, m1)

    for ci in range(cs):
        rows = slice(RET_CHUNK * ci, RET_CHUNK * (ci + 1))
        q = q_ref[rows, :]
        qf32 = q.astype(f32)
        q_f = (qf32 * qdec_f).astype(bf16)
        q_b = (qf32 * qdec_b).astype(bf16)
        for hp in range(N_PAIRS):
            sl = slice(LANES * hp, LANES * (hp + 1))
            q2 = q[:, sl]
            k2 = k_ref[rows, sl]
            v2 = v_ref[rows, sl]
            a = lax.dot_general(q2, _per_head_rows(k2, lo_half), NT_DIMS, preferred_element_type=f32)
            lhs = [(a[:, :RET_CHUNK] * decay[2 * hp]).astype(bf16), (a[:, RET_CHUNK:] * decay[2 * hp + 1]).astype(bf16),
                   q_f[:, sl], q_b[:, sl]]
            rhs = [_per_head_rows(v2, lo_half), sf_ref[ci, hp], sb_ref[ci, hp]]
            y = jnp.dot(jnp.concatenate(lhs, axis=1), jnp.concatenate(rhs, axis=0), preferred_element_type=f32)
            d = y - head_mean(y)
            yn = d * lax.rsqrt(head_mean(d * d) + EPS) * gn_ref[:, sl]
            o_ref[rows, sl] = (_silu(gate_ref[rows, sl]) * yn).astype(bf16)


def retention_outputs(ret, gate, sf, sb, lgf, lgb, lgf_lane, lgb_lane, gn):
    L = ret.shape[0]
    N = L // RET_CHUNK
    cs = min(RET_STEP, N)
    col = lambda c: pl.BlockSpec((cs * RET_CHUNK, RET_DIM), lambda n, *_: (n, c))
    vec = pl.BlockSpec((1, RET_DIM), lambda n, *_: (0, 0))
    st = pl.BlockSpec((cs, N_PAIRS, LANES, LANES), lambda n, *_: (n, 0, 0, 0))
    return pl.pallas_call(
        functools.partial(_ret_out_kernel, cs=cs),
        grid_spec=pltpu.PrefetchScalarGridSpec(
            num_scalar_prefetch=2,
            grid=(N // cs,),
            in_specs=[col(0), col(1), col(2), col(0), st, st, vec, vec, vec],
            out_specs=pl.BlockSpec((cs * RET_CHUNK, RET_DIM), lambda n, *_: (n, 0)),
        ),
        out_shape=jax.ShapeDtypeStruct((L, RET_DIM), bf16),
        compiler_params=_params("parallel"),
        name="retention_outputs",
    )(lgf, lgb, ret, ret, ret, gate, sf, sb, lgf_lane, lgb_lane, gn)


FFN_HALO = 16
FFN_TF = 256
POOL_IN_HALO = FFN_HALO + POOL_HALO
POOL_IN_BLOCK = 2 * FFN_HALO


def _out_ffn_kernel(hp_ref, h_ref, hn_ref, pp_ref, p_ref, pn_ref, np_ref, n_ref, nn_ref, rp_ref, r_ref, rn_ref,
                    mod_ref, g_ref, wp_ref, ps_ref, wo_ref, wu_ref, cw_ref, cb_ref, wd_ref, fg_ref, o_ref,
                    aext, gbuf, lvl, *, tm, L, final):
    i = pl.program_id(0)
    ni = pl.num_programs(0)
    ext = tm + 2 * FFN_HALO

    def rows3(prev, cur, nxt):
        return jnp.concatenate([prev[...], cur[...], nxt[...]], axis=0)

    y = jnp.dot(jnp.concatenate([rows3(np_ref, n_ref, nn_ref), rows3(rp_ref, r_ref, rn_ref)], axis=1),
                wo_ref[POOL_DIM:, :], preferred_element_type=f32)
    pool_x = _pooled_rows(pp_ref[POOL_IN_BLOCK - POOL_IN_HALO:, :], p_ref, pn_ref[:POOL_IN_HALO, :], wp_ref, ps_ref,
                          lvl, first=i == 0, last=i == ni - 1, row0=i * tm, L=L)
    y = y + jnp.dot(pool_x, wo_ref[:POOL_DIM, :], preferred_element_type=f32)
    hx = rows3(hp_ref, h_ref, hn_ref) + mod_ref[2:3, :] * y
    o_ref[...] = hx[FFN_HALO:FFN_HALO + tm]
    a = hx * lax.rsqrt(jnp.mean(hx * hx, axis=-1, keepdims=True) + EPS) * g_ref[...]
    a = a * (1.0 + mod_ref[4:5, :]) + mod_ref[3:4, :]
    row = lax.broadcasted_iota(jnp.int32, (ext, 1), 0)
    inside = jnp.logical_and(jnp.logical_or(row >= FFN_HALO, i > 0), jnp.logical_or(row < FFN_HALO + tm, i < ni - 1))
    aext[...] = jnp.where(inside, a, 0.0).astype(bf16)
    ae = aext[...]

    def branch(lo):
        u = jnp.dot(ae, wu_ref[:, lo:lo + FFN_TF], preferred_element_type=f32)
        up = pltpu.roll(u, 1, 0)[FFN_HALO:FFN_HALO + tm]
        un = pltpu.roll(u, ext - 1, 0)[FFN_HALO:FFN_HALO + tm]
        uc = u[FFN_HALO:FFN_HALO + tm]
        cw = cw_ref[:, lo:lo + FFN_TF]
        return up * cw[0:1, :] + uc * cw[1:2, :] + un * cw[2:3, :] + cb_ref[:, lo:lo + FFN_TF]

    for s in range(D_FF // FFN_TF):
        lo = FFN_TF * s
        gbuf[:, lo:lo + FFN_TF] = (_silu(branch(lo)) * branch(D_FF + lo)).astype(bf16)
    y = jnp.dot(gbuf[...], wd_ref[...], preferred_element_type=f32)
    out = o_ref[...] + mod_ref[5:6, :] * y
    if final:
        out = out * lax.rsqrt(jnp.mean(out * out, axis=-1, keepdims=True) + EPS) * fg_ref[...]
    o_ref[...] = out


def out_proj_ffn(h, p, na_x, ret_x, mod, g2, w_pool, pool_scale, w_out, w_up, conv_w, conv_b, w_down, final_g, final):
    L = h.shape[0]
    tm = _row_tile(L)

    def halo3(n, halo=FFN_HALO):
        hb = tm // halo
        return [
            pl.BlockSpec((halo, n), lambda i: (jnp.maximum(i * hb - 1, 0), 0)),
            pl.BlockSpec((tm, n), lambda i: (i, 0)),
            pl.BlockSpec((halo, n), lambda i: (jnp.minimum((i + 1) * hb, L // halo - 1), 0)),
        ]

    const = lambda arr: pl.BlockSpec(arr.shape, lambda i: (0,) * arr.ndim, pipeline_mode=pl.Buffered(1))
    consts = (mod, g2, w_pool, pool_scale, w_out, w_up, conv_w, conv_b, w_down, final_g)
    return pl.pallas_call(
        functools.partial(_out_ffn_kernel, tm=tm, L=L, final=final),
        grid=(L // tm,),
        in_specs=(halo3(D_MODEL) + halo3(POOL_DIM, POOL_IN_BLOCK) + halo3(NA_DIM) + halo3(RET_DIM)
                  + [const(a) for a in consts]),
        out_specs=pl.BlockSpec((tm, D_MODEL), lambda i: (i, 0)),
        out_shape=jax.ShapeDtypeStruct((L, D_MODEL), f32),
        scratch_shapes=[pltpu.VMEM((tm + 2 * FFN_HALO, D_MODEL), bf16), pltpu.VMEM((tm, D_FF), bf16),
                        pltpu.VMEM((4, tm + 2 * (POOL_IN_HALO + POOL_MARGIN), POOL_DIM), f32)],
        compiler_params=_params("parallel", fuse_inputs=[False] * 12 + [any(a is w for w in (w_out, w_up, w_down))
                                                                       for a in consts]),
        name="out_proj_ffn",
    )(h, h, h, p, p, p, na_x, na_x, na_x, ret_x, ret_x, ret_x, *consts)


def kernel(x, c, ctx, c_ctx, w_mod, b_mod, norm1_g, w_in, pool_w, pool_scale, na_rpb, ret_decay_fwd, ret_decay_bwd,
           ret_gn_g, w_out, norm2_g, w_up, conv_w, conv_b, w_down, final_g):
    B, L, _ = x.shape
    assert B == 1
    n_ctx = ctx.shape[1]
    h = x[0]
    hc = ctx[0]
    mods = mod_vectors(c, c_ctx, w_mod, b_mod)
    rope = rope_tables(L // GRID_W)
    no_rope = rope_identity(n_ctx // GRID_W)
    zero_state = jnp.zeros((N_PAIRS, LANES, LANES), f32)
    fg = final_g.reshape(1, D_MODEL)
    stacks = (w_in, w_out, w_up, w_down)
    wi, wo, wu, wd = [stack[0].astype(bf16) for stack in stacks]
    bias_rows = na_bias_rows(na_rpb)
    for l in range(DEPTH):
        need_ctx = l < DEPTH - 1
        mx = mods[l, 0].reshape(6, D_MODEL)
        mc = mods[l, 1].reshape(6, D_MODEL)
        g1 = norm1_g[l].reshape(1, D_MODEL)
        g2 = norm2_g[l].reshape(1, D_MODEL)
        cb = conv_b[l].reshape(1, 2 * D_FF)
        wp = pool_blockdiag(pool_w[l])
        ps = pool_scale[l].reshape(1, POOL_DIM)
        lgf = ret_decay_fwd[l].astype(f32)
        lgb = ret_decay_bwd[l].astype(f32)
        lgf_lane = jnp.repeat(lgf, HEAD_DIM).reshape(1, RET_DIM)
        lgb_lane = jnp.repeat(lgb, HEAD_DIM).reshape(1, RET_DIM)
        gn = ret_gn_g[l].reshape(1, RET_DIM)
        ffn_w = (g2, wp, ps, wo, wu, conv_w[l], cb, wd, fg)

        pc, nac, retc, gatec = in_proj(hc, g1, mc, wi, no_rope)
        cast = [(stack, l + 1) for stack in stacks] if need_ctx else []
        px, nax, retx, gatex, *next_w = in_proj(h, g1, mx, wi, rope, cast)

        sfc, sbc, sf0, sb0 = retention_states(retc, lgf_lane, lgb_lane, zero_state, zero_state)
        sfx, sbx, _, _ = retention_states(retx, lgf_lane, lgb_lane, sf0, sb0)

        na_x = neighborhood_attention(nax, nac, bias_rows[l])
        ret_x = retention_outputs(retx, gatex, sfx, sbx, lgf, lgb, lgf_lane, lgb_lane, gn)
        h = out_proj_ffn(h, px, na_x, ret_x, mx, *ffn_w, final=not need_ctx)

        if need_ctx:
            na_c = context_attention(nac)
            ret_c = retention_outputs(retc, gatec, sfc, sbc, lgf, lgb, lgf_lane, lgb_lane, gn)
            hc = out_proj_ffn(hc, pc, na_c, ret_c, mc, *ffn_w, final=False)
            wi, wo, wu, wd = next_w
    return h[None]
```
